```python
import math
import jax, jax.numpy as jnp
from jax import lax
import numpy as np

D_MODEL = 1024
BATCH = 8
SEQ = 4096
DEPTH = 2

CHUNK = 64
Q_BLOCK = 128
RMS_EPS = 1e-6

N_BUCKETS = 32
MAX_DISTANCE = 128

A_HEADS = 4
A_HEAD_DIM = 64
A_V_DIM = 2 * A_HEAD_DIM
A_WIDTH = A_HEADS * A_V_DIM

B_HEADS = 8
B_HEAD_DIM = 64
B_WIDTH = B_HEADS * B_HEAD_DIM
IDX_HEADS = 8
IDX_DIM = 64
TOPK_MAX = 256

BIAS_HEADS = A_HEADS + B_HEADS
L0_WIDTH = A_WIDTH + B_WIDTH
L0_SIZES = (A_HEADS * 2 * A_HEAD_DIM, A_HEADS * 2 * A_HEAD_DIM, A_WIDTH,
            B_WIDTH, B_HEAD_DIM, B_HEAD_DIM,
            IDX_HEADS * IDX_DIM, IDX_DIM, IDX_HEADS,
            L0_WIDTH)
L0_COLS = sum(L0_SIZES)

C_HEADS = 16
C_NOPE = 64
C_ROPE = 32
C_V = 64
Q_LORA = 384
KV_LORA = 256
C_WIDTH = C_HEADS * C_V
ROPE_THETA = 10000.0
L1_SIZES = (Q_LORA, KV_LORA, C_ROPE, C_WIDTH)
L1_COLS = sum(L1_SIZES)

N_EVEN = (DEPTH + 1) // 2
N_ODD = DEPTH // 2

kernel_name = 'hybrid_diff_dsa_mla_stream_encoder'


def rms_norm(x, g):
    xf = x.astype(jnp.float32)
    y = xf * lax.rsqrt(jnp.mean(xf * xf, axis=-1, keepdims=True) + RMS_EPS)
    return (y * g.astype(jnp.float32)).astype(x.dtype)


def split_cols(t, sizes):
    return jnp.split(t, np.cumsum(sizes)[:-1].tolist(), axis=-1)


def to_blocks(t):
    b, s = t.shape[:2]
    return t.reshape((b, s // Q_BLOCK, Q_BLOCK) + t.shape[2:]).swapaxes(0, 1)


def from_blocks(o):
    nb, b, q = o.shape[:3]
    return o.swapaxes(0, 1).reshape((b, nb * q) + o.shape[3:])


def chunk_mask(qpos, kpos):
    return (kpos[None, :] // CHUNK) <= (qpos[:, None] // CHUNK)


def t5_bucket(rel):
    nb = N_BUCKETS // 2
    ret = jnp.where(rel > 0, nb, 0)
    n = jnp.abs(rel)
    max_exact = nb // 2
    n_f = jnp.maximum(n, 1).astype(jnp.float32)
    large = max_exact + (jnp.log(n_f / max_exact) / math.log(MAX_DISTANCE / max_exact)
                         * (nb - max_exact)).astype(jnp.int32)
    large = jnp.minimum(large, nb - 1)
    return ret + jnp.where(n < max_exact, n, large)


def rope_tables(s):
    inv = ROPE_THETA ** (-jnp.arange(0, C_ROPE, 2, dtype=jnp.float32) / C_ROPE)
    ang = jnp.arange(s, dtype=jnp.float32)[:, None] * inv[None, :]
    return jnp.cos(ang), jnp.sin(ang)


def apply_rope(x, cos, sin):
    x1, x2 = jnp.split(x, 2, axis=-1)
    c = cos.astype(x.dtype)
    s = sin.astype(x.dtype)
    return jnp.concatenate([x1 * c - x2 * s, x1 * s + x2 * c], axis=-1)


def diff_attention(q, k, v, lam, bias_tab):
    s_len = q.shape[1]
    nb = s_len // Q_BLOCK
    kpos = jnp.arange(s_len)
    scale = A_HEAD_DIM ** -0.5

    def block(args):
        q_blk, bi = args
        qpos = bi * Q_BLOCK + jnp.arange(Q_BLOCK)
        s = jnp.einsum('bqhmd,bkhmd->bhmqk', q_blk, k).astype(jnp.float32) * scale
        bias = bias_tab[t5_bucket(kpos[None, :] - qpos[:, None])].astype(jnp.float32)
        s = s + jnp.transpose(bias, (2, 0, 1))[None, :, None]
        s = jnp.where(chunk_mask(qpos, kpos), s, -jnp.inf)
        p = jax.nn.softmax(s, axis=-1)
        p = p[:, :, 0] - lam * p[:, :, 1]
        return jnp.einsum('bhqk,bkhe->bqhe', p.astype(v.dtype), v)

    return from_blocks(lax.map(block, (to_blocks(q), jnp.arange(nb))))


def dsa_attention(q, k, v, q_idx, k_idx, w_idx, bias_tab, top_k):
    b, s_len = q.shape[:2]
    nb = s_len // Q_BLOCK
    kpos = jnp.arange(s_len)
    scale = B_HEAD_DIM ** -0.5

    def block(args):
        q_blk, qi_blk, w_blk, bi = args
        qpos = bi * Q_BLOCK + jnp.arange(Q_BLOCK)
        logits = jnp.einsum('bqhd,bkd->bqhk', qi_blk, k_idx).astype(jnp.float32)
        score = jnp.einsum('bqh,bqhk->bqk', w_blk.astype(jnp.float32), jax.nn.relu(logits))
        score = jnp.where(chunk_mask(qpos, kpos)[None], score, -jnp.inf)
        _, sel = lax.top_k(score, top_k)
        valid = (sel // CHUNK) <= (qpos // CHUNK)[None, :, None]
        flat = sel.reshape(b, -1, 1)
        k_sel = jnp.take_along_axis(k, flat, axis=1).reshape(b, Q_BLOCK, top_k, B_HEAD_DIM)
        v_sel = jnp.take_along_axis(v, flat, axis=1).reshape(b, Q_BLOCK, top_k, B_HEAD_DIM)
        s = jnp.einsum('bqhd,bqkd->bhqk', q_blk, k_sel).astype(jnp.float32) * scale
        bias = bias_tab[t5_bucket(sel - qpos[None, :, None])].astype(jnp.float32)
        s = s + jnp.transpose(bias, (0, 3, 1, 2))
        s = jnp.where(valid[:, None], s, -jnp.inf)
        p = jax.nn.softmax(s, axis=-1)
        return jnp.einsum('bhqk,bqkd->bqhd', p.astype(v.dtype), v_sel)

    xs = (to_blocks(q), to_blocks(q_idx), to_blocks(w_idx), jnp.arange(nb))
    return from_blocks(lax.map(block, xs))


def mla_attention(q_nope, q_rope, k_nope, k_rope, v):
    s_len = q_nope.shape[1]
    nb = s_len // Q_BLOCK
    kpos = jnp.arange(s_len)
    scale = (C_NOPE + C_ROPE) ** -0.5

    def block(args):
        qn, qr, bi = args
        qpos = bi * Q_BLOCK + jnp.arange(Q_BLOCK)
        s = (jnp.einsum('bqhd,bkhd->bhqk', qn, k_nope)
             + jnp.einsum('bqhr,bkr->bhqk', qr, k_rope)).astype(jnp.float32) * scale
        s = jnp.where(chunk_mask(qpos, kpos), s, -jnp.inf)
        p = jax.nn.softmax(s, axis=-1)
        return jnp.einsum('bhqk,bkhd->bqhd', p.astype(v.dtype), v)

    xs = (to_blocks(q_nope), to_blocks(q_rope), jnp.arange(nb))
    return from_blocks(lax.map(block, xs))


def diff_dsa_layer(x, norm_g, w_in, lq1, lk1, lq2, lk2, subln_g, w_o, rel_bias, layer, top_k):
    b, s_len, _ = x.shape
    h = rms_norm(x, norm_g)
    proj = h @ w_in
    qa, ka, va, qb, kb, vb, qi, ki, wi, gate = split_cols(proj, L0_SIZES)
    lam_init = 0.8 - 0.6 * math.exp(-0.3 * layer)
    f32 = jnp.float32
    lam = (jnp.exp(jnp.sum(lq1.astype(f32) * lk1.astype(f32)))
           - jnp.exp(jnp.sum(lq2.astype(f32) * lk2.astype(f32))) + lam_init)
    o_a = diff_attention(qa.reshape(b, s_len, A_HEADS, 2, A_HEAD_DIM),
                         ka.reshape(b, s_len, A_HEADS, 2, A_HEAD_DIM),
                         va.reshape(b, s_len, A_HEADS, A_V_DIM),
                         lam, rel_bias[:, :A_HEADS])
    o_a = rms_norm(o_a, subln_g) * (1.0 - lam_init)
    o_b = dsa_attention(qb.reshape(b, s_len, B_HEADS, B_HEAD_DIM), kb, vb,
                        qi.reshape(b, s_len, IDX_HEADS, IDX_DIM), ki, wi,
                        rel_bias[:, A_HEADS:], top_k)
    mix = jnp.concatenate([o_a.reshape(b, s_len, A_WIDTH), o_b.reshape(b, s_len, B_WIDTH)], axis=-1)
    return x + (mix * jax.nn.silu(gate)) @ w_o


def mla_layer(x, norm_g, w_in, q_norm, w_uq, kv_norm, w_ukv, w_o, cos, sin):
    b, s_len, _ = x.shape
    h = rms_norm(x, norm_g)
    proj = h @ w_in
    q_lat, kv_lat, k_rope, gate = split_cols(proj, L1_SIZES)
    q = (rms_norm(q_lat, q_norm) @ w_uq).reshape(b, s_len, C_HEADS, C_NOPE + C_ROPE)
    kv = (rms_norm(kv_lat, kv_norm) @ w_ukv).reshape(b, s_len, C_HEADS, C_NOPE + C_V)
    q_nope, q_rope = q[..., :C_NOPE], q[..., C_NOPE:]
    k_nope, v = kv[..., :C_NOPE], kv[..., C_NOPE:]
    q_rope = apply_rope(q_rope, cos[:, None, :], sin[:, None, :])
    k_rope = apply_rope(k_rope, cos, sin)
    o = mla_attention(q_nope, q_rope, k_nope, k_rope, v).reshape(b, s_len, C_WIDTH)
    return x + (o * jax.nn.silu(gate)) @ w_o


def setup_inputs(seed: int = 0) -> dict:
    key = jax.random.key(seed)
    ks = jax.random.split(key, 20)
    f32 = jnp.float32

    def dense(k, shape, fan_in):
        return jax.random.normal(k, shape, f32) * (fan_in ** -0.5)

    def gain(k, shape):
        return 1.0 + 0.02 * jax.random.normal(k, shape, f32)

    return {
        'x': jax.random.normal(ks[0], (BATCH, SEQ, D_MODEL), f32),
        'rel_bias': 0.2 * jax.random.normal(ks[1], (N_BUCKETS, BIAS_HEADS), f32),
        'e_norm': gain(ks[2], (N_EVEN, D_MODEL)),
        'e_w_in': dense(ks[3], (N_EVEN, D_MODEL, L0_COLS), D_MODEL),
        'e_lam_q1': 0.1 * jax.random.normal(ks[4], (N_EVEN, A_HEAD_DIM), f32),
        'e_lam_k1': 0.1 * jax.random.normal(ks[5], (N_EVEN, A_HEAD_DIM), f32),
        'e_lam_q2': 0.1 * jax.random.normal(ks[6], (N_EVEN, A_HEAD_DIM), f32),
        'e_lam_k2': 0.1 * jax.random.normal(ks[7], (N_EVEN, A_HEAD_DIM), f32),
        'e_subln': gain(ks[8], (N_EVEN, A_V_DIM)),
        'e_w_o': dense(ks[9], (N_EVEN, L0_WIDTH, D_MODEL), L0_WIDTH),
        'o_norm': gain(ks[10], (N_ODD, D_MODEL)),
        'o_w_in': dense(ks[11], (N_ODD, D_MODEL, L1_COLS), D_MODEL),
        'o_q_norm': gain(ks[12], (N_ODD, Q_LORA)),
        'o_w_uq': dense(ks[13], (N_ODD, Q_LORA, C_HEADS * (C_NOPE + C_ROPE)), Q_LORA),
        'o_kv_norm': gain(ks[14], (N_ODD, KV_LORA)),
        'o_w_ukv': dense(ks[15], (N_ODD, KV_LORA, C_HEADS * (C_NOPE + C_V)), KV_LORA),
        'o_w_o': dense(ks[16], (N_ODD, C_WIDTH, D_MODEL), C_WIDTH),
        'final_norm': gain(ks[17], (D_MODEL,)),
    }


def reference(x, rel_bias, e_norm, e_w_in, e_lam_q1, e_lam_k1, e_lam_q2, e_lam_k2, e_subln, e_w_o,
              o_norm, o_w_in, o_q_norm, o_w_uq, o_kv_norm, o_w_ukv, o_w_o, final_norm):
    s_len = x.shape[1]
    top_k = min(TOPK_MAX, s_len // 4)
    cos, sin = rope_tables(s_len)
    h = x
    for layer in range(DEPTH):
        i = layer // 2
        if layer % 2 == 0:
            h = diff_dsa_layer(h, e_norm[i], e_w_in[i], e_lam_q1[i], e_lam_k1[i], e_lam_q2[i],
                               e_lam_k2[i], e_subln[i], e_w_o[i], rel_bias, layer, top_k)
        else:
            h = mla_layer(h, o_norm[i], o_w_in[i], o_q_norm[i], o_w_uq[i], o_kv_norm[i],
                          o_w_ukv[i], o_w_o[i], cos, sin)
    return rms_norm(h, final_norm)
```

```python
import functools
import math

import numpy as np
import jax
import jax.numpy as jnp
from jax import lax
from jax.experimental import pallas as pl
from jax.experimental.pallas import tpu as pltpu

F32 = jnp.float32
BF16 = jnp.bfloat16

CHUNK = 64
RMS_EPS = 1e-6
N_BUCKETS = 32
MAX_DISTANCE = 128
A_HEADS = 4
A_HEAD_DIM = 64
A_V_DIM = 2 * A_HEAD_DIM
B_HEADS = 8
B_HEAD_DIM = 64
IDX_HEADS = 8
IDX_DIM = 64
TOPK_MAX = 256
C_HEADS = 16
C_NOPE = 64
C_ROPE = 32
C_V = 64
Q_LORA = 384
KV_LORA = 256
ROPE_THETA = 10000.0

LANES = 128
ATT_TILE = 256
ROW_TILE = 512
VMEM_LIMIT = 48 * 1024 * 1024

NEG = -1e30
INT_MIN = -(2 ** 31)
NT_DIMS = (((1,), (1,)), ((), ()))


def _t5_thresholds():
    nb = N_BUCKETS // 2
    max_exact = nb // 2
    n = np.arange(max_exact, MAX_DISTANCE + 1, dtype=np.float64)
    large = max_exact + (np.log(n / max_exact) / math.log(MAX_DISTANCE / max_exact)
                         * (nb - max_exact)).astype(np.int64)
    large = np.minimum(large, nb - 1)
    thr = [int(n[np.argmax(large >= b)]) for b in range(max_exact + 1, nb)]
    return max_exact, thr


T5_MAX_EXACT, T5_THRESHOLDS = _t5_thresholds()
T5_FAR_BUCKET = N_BUCKETS // 2 - 1


def _cparams(sem):
    return pltpu.CompilerParams(dimension_semantics=sem, vmem_limit_bytes=VMEM_LIMIT)


def _rms(x, g):
    return x * lax.rsqrt(jnp.mean(x * x, axis=-1, keepdims=True) + RMS_EPS) * g


def _silu(g):
    return g / (1.0 + jnp.exp(-g))


def _chunk_visible(shape):
    shift = CHUNK.bit_length() - 1
    assert 1 << shift == CHUNK
    row = lax.broadcasted_iota(jnp.int32, shape, 0)
    col = lax.broadcasted_iota(jnp.int32, shape, 1)
    return (col >> shift) <= (row >> shift)


def _bias_tile_kernel(tab_ref, o_ref):
    h = pl.program_id(0)
    t = pl.program_id(1)
    shape = (ATT_TILE, ATT_TILE)
    row = lax.broadcasted_iota(jnp.int32, shape, 0)
    col = lax.broadcasted_iota(jnp.int32, shape, 1)
    rel = col - row - t * ATT_TILE
    n = jnp.abs(rel)
    large = jnp.full(shape, T5_MAX_EXACT, jnp.int32)
    for thr in T5_THRESHOLDS:
        large = large + jnp.where(n >= thr, 1, 0)
    bucket = jnp.where(rel > 0, N_BUCKETS // 2, 0) + jnp.where(n < T5_MAX_EXACT, n, large)
    val = jnp.full(shape, tab_ref[0, h], F32)
    for b in range(1, N_BUCKETS):
        val = jnp.where(bucket == b, tab_ref[b, h], val)
    visible = _chunk_visible(shape) | (t > 0)
    o_ref[0, 0] = jnp.where(visible, val, NEG)


def _bias_tiles(rel_bias):
    heads = rel_bias.shape[1]
    return pl.pallas_call(
        _bias_tile_kernel,
        grid=(heads, 2),
        in_specs=[pl.BlockSpec(memory_space=pltpu.SMEM)],
        out_specs=pl.BlockSpec((1, 1, ATT_TILE, ATT_TILE), lambda h, t: (h, t, 0, 0)),
        out_shape=jax.ShapeDtypeStruct((heads, 2, ATT_TILE, ATT_TILE), F32),
        compiler_params=_cparams(("arbitrary", "arbitrary")),
    )(rel_bias)


def _proj0_kernel(x_ref, g_ref, w_main_ref, w_small_ref, w_gate_ref,
                  qa_ref, ka_ref, va_ref, qb_ref, qi_ref, kb_ref, vb_ref, ki_ref, wi_ref, gate_ref):
    hn = _rms(x_ref[...], g_ref[...]).astype(BF16)
    main = (qa_ref, ka_ref, va_ref, qb_ref, qi_ref)
    for i, o_ref in enumerate(main):
        w = w_main_ref[:, i * 512:(i + 1) * 512]
        o_ref[...] = jnp.dot(hn, w, preferred_element_type=F32).astype(o_ref.dtype)
    small = (kb_ref, vb_ref, ki_ref, wi_ref)
    for i, o_ref in enumerate(small):
        w = w_small_ref[:, i * LANES:(i + 1) * LANES]
        o_ref[...] = jnp.dot(hn, w, preferred_element_type=F32).astype(o_ref.dtype)
    gate_ref[...] = jnp.dot(hn, w_gate_ref[...], preferred_element_type=F32).astype(gate_ref.dtype)


def _proj0(x2, g, w_main, w_small, w_gate):
    n, d = x2.shape
    rows = lambda i: (i, 0)
    const = lambda i: (0, 0)
    outs = [(512, BF16)] * 5 + [(LANES, BF16)] * 3 + [(LANES, F32), (w_gate.shape[1], BF16)]
    return pl.pallas_call(
        _proj0_kernel,
        grid=(n // ROW_TILE,),
        in_specs=[pl.BlockSpec((ROW_TILE, d), rows), pl.BlockSpec((1, d), const),
                  pl.BlockSpec(w_main.shape, const), pl.BlockSpec(w_small.shape, const),
                  pl.BlockSpec(w_gate.shape, const)],
        out_specs=[pl.BlockSpec((ROW_TILE, c), rows) for c, _ in outs],
        out_shape=[jax.ShapeDtypeStruct((n, c), dt) for c, dt in outs],
        compiler_params=_cparams(("arbitrary",)),
    )(x2, g, w_main, w_small, w_gate)


def _proj1_kernel(x_ref, g_ref, cos_ref, sin_ref, w_lat_ref, w_gate_ref, qn_ref, kvn_ref,
                  w_q_ref, w_qrot_ref, w_k_ref, w_v_ref, q_ref, k_ref, v_ref, gate_ref):
    hn = _rms(x_ref[...], g_ref[...]).astype(BF16)
    cos = cos_ref[...]
    sin = sin_ref[...]
    lat = jnp.dot(hn, w_lat_ref[...], preferred_element_type=F32)
    q_lat = lat[:, :Q_LORA]
    kv_lat = lat[:, Q_LORA:Q_LORA + KV_LORA]
    kr = lat[:, Q_LORA + KV_LORA:Q_LORA + KV_LORA + LANES]
    kr_rot = lat[:, Q_LORA + KV_LORA + LANES:]
    k_rope = kr * cos + kr_rot * sin
    qn = _rms(q_lat, qn_ref[...]).astype(BF16)
    kvn = _rms(kv_lat, kvn_ref[...]).astype(BF16)
    for h in range(C_HEADS):
        sl = slice(h * LANES, (h + 1) * LANES)
        qa = jnp.dot(qn, w_q_ref[:, sl], preferred_element_type=F32)
        qr = jnp.dot(qn, w_qrot_ref[:, sl], preferred_element_type=F32)
        q_ref[:, sl] = (qa * cos + qr * sin).astype(q_ref.dtype)
        kn = jnp.dot(kvn, w_k_ref[:, sl], preferred_element_type=F32)
        k_ref[:, sl] = (kn + k_rope).astype(k_ref.dtype)
    v_ref[...] = jnp.dot(kvn, w_v_ref[...], preferred_element_type=F32).astype(v_ref.dtype)
    gate_ref[...] = jnp.dot(hn, w_gate_ref[...], preferred_element_type=F32).astype(gate_ref.dtype)


def _proj1(x2, g, cos_t, sin_t, w_lat, w_gate, qn_g, kvn_g, w_q, w_qrot, w_k, w_v):
    n, d = x2.shape
    seq_tiles = cos_t.shape[0] // ROW_TILE
    rows = lambda i: (i, 0)
    const = lambda i: (0, 0)
    pos = lambda i: (i % seq_tiles, 0)
    full = lambda a: pl.BlockSpec(a.shape, const)
    outs = [(C_HEADS * LANES, BF16), (C_HEADS * LANES, BF16), (C_HEADS * C_V, BF16),
            (w_gate.shape[1], BF16)]
    return pl.pallas_call(
        _proj1_kernel,
        grid=(n // ROW_TILE,),
        in_specs=[pl.BlockSpec((ROW_TILE, d), rows), full(g),
                  pl.BlockSpec((ROW_TILE, LANES), pos), pl.BlockSpec((ROW_TILE, LANES), pos),
                  full(w_lat), full(w_gate), full(qn_g), full(kvn_g),
                  full(w_q), full(w_qrot), full(w_k), full(w_v)],
        out_specs=[pl.BlockSpec((ROW_TILE, c), rows) for c, _ in outs],
        out_shape=[jax.ShapeDtypeStruct((n, c), dt) for c, dt in outs],
        compiler_params=_cparams(("arbitrary",)),
    )(x2, g, cos_t, sin_t, w_lat, w_gate, qn_g, kvn_g, w_q, w_qrot, w_k, w_v)


def _out_kernel(*refs, n_mix, final_norm):
    mix_refs = refs[:n_mix]
    gate_ref, x_ref, w_ref = refs[n_mix:n_mix + 3]
    rest = refs[n_mix + 3:]
    sg = _silu(gate_ref[...].astype(F32))
    y = x_ref[...]
    off = 0
    for m_ref in mix_refs:
        c = m_ref.shape[1]
        a = (m_ref[...].astype(F32) * sg[:, off:off + c]).astype(BF16)
        y = y + jnp.dot(a, w_ref[off:off + c, :], preferred_element_type=F32)
        off += c
    if final_norm:
        g_ref, o_ref = rest
        o_ref[...] = _rms(y, g_ref[...])
    else:
        (o_ref,) = rest
        o_ref[...] = y


def _out_proj(mixes, gate, x2, w_o, final_g=None):
    n, d = x2.shape
    rows = lambda i: (i, 0)
    const = lambda i: (0, 0)
    in_specs = [pl.BlockSpec((ROW_TILE, m.shape[1]), rows) for m in mixes]
    in_specs += [pl.BlockSpec((ROW_TILE, gate.shape[1]), rows), pl.BlockSpec((ROW_TILE, d), rows),
                 pl.BlockSpec(w_o.shape, const)]
    args = list(mixes) + [gate, x2, w_o]
    if final_g is not None:
        in_specs.append(pl.BlockSpec((1, d), const))
        args.append(final_g)
    return pl.pallas_call(
        functools.partial(_out_kernel, n_mix=len(mixes), final_norm=final_g is not None),
        grid=(n // ROW_TILE,),
        in_specs=in_specs,
        out_specs=pl.BlockSpec((ROW_TILE, d), rows),
        out_shape=jax.ShapeDtypeStruct((n, d), F32),
        compiler_params=_cparams(("arbitrary",)),
    )(*args)


def _flash_update(s, v, m_ref, l_ref, acc_ref, shift):
    m_old = m_ref[...]
    m_new = jnp.maximum(m_old, jnp.max(s, axis=-1, keepdims=True) + shift)
    p = jnp.exp(s - (m_new - shift))
    alpha = jnp.exp(m_old - m_new)
    l_ref[...] = alpha * l_ref[...] + jnp.sum(p, axis=-1, keepdims=True)
    acc_ref[...] = alpha * acc_ref[...] + jnp.dot(p.astype(v.dtype), v, preferred_element_type=F32)
    m_ref[...] = m_new


def _init_flash(m_ref, l_ref, acc_ref):
    m_ref[...] = jnp.full(m_ref.shape, -jnp.inf, F32)
    l_ref[...] = jnp.zeros(l_ref.shape, F32)
    acc_ref[...] = jnp.zeros(acc_ref.shape, F32)


def _half_lane_mask():
    return lax.broadcasted_iota(jnp.int32, (1, LANES), 1) < (LANES // 2)


def _split_halves(kt, lo):
    zero = jnp.zeros_like(kt)
    return jnp.where(lo, kt, zero), jnp.where(lo, zero, kt)


def _key_tile_loop(qt, far_fn, near_fn, diag_fn):
    def far_body(j, carry):
        far_fn(j)
        return carry
    lax.fori_loop(0, jnp.maximum(qt - 1, 0), far_body, 0)

    @pl.when(qt >= 1)
    def _():
        near_fn(qt - 1)

    diag_fn(qt)


def _attn_a_kernel(cfar_ref, q_ref, k_ref, v_ref, bias_ref, lam_ref, g_ref, o_ref,
                   m_ref, l_ref, acc_ref, *, lam_init):
    h = pl.program_id(1)
    qt = pl.program_id(2)
    lo = _half_lane_mask()
    q = q_ref[0]
    _init_flash(m_ref, l_ref, acc_ref)

    def tile(j, bias_idx):
        k0 = pl.multiple_of(j * ATT_TILE, ATT_TILE)
        k1, k2 = _split_halves(k_ref[0, pl.ds(k0, ATT_TILE), :], lo)
        vt = v_ref[0, pl.ds(k0, ATT_TILE), :]
        for mp, kk in enumerate((k1, k2)):
            s = lax.dot_general(q, kk, NT_DIMS, preferred_element_type=F32)
            if bias_idx is None:
                shift = cfar_ref[h]
            else:
                s = s + bias_ref[0, bias_idx]
                shift = 0.0
            _flash_update(s, vt, m_ref.at[mp], l_ref.at[mp], acc_ref.at[mp], shift)

    _key_tile_loop(qt, lambda j: tile(j, None), lambda j: tile(j, 1), lambda j: tile(j, 0))

    lam_v = lam_ref[...]
    lam = (jnp.exp(jnp.sum(lam_v[0:1] * lam_v[1:2], axis=-1, keepdims=True))
           - jnp.exp(jnp.sum(lam_v[2:3] * lam_v[3:4], axis=-1, keepdims=True)) + lam_init)
    o = acc_ref[0] / l_ref[0] - lam * (acc_ref[1] / l_ref[1])
    o_ref[0] = (_rms(o, g_ref[...]) * (1.0 - lam_init)).astype(o_ref.dtype)


def _attn_a(cfar, qa, ka, va, bias, lam_vecs, subln_g, lam_init):
    b, s, _ = qa.shape
    nq = s // ATT_TILE
    return pl.pallas_call(
        functools.partial(_attn_a_kernel, lam_init=lam_init),
        grid=(b, A_HEADS, nq),
        in_specs=[pl.BlockSpec(memory_space=pltpu.SMEM),
                  pl.BlockSpec((1, ATT_TILE, LANES), lambda bi, h, qi: (bi, qi, h)),
                  pl.BlockSpec((1, s, LANES), lambda bi, h, qi: (bi, 0, h)),
                  pl.BlockSpec((1, s, LANES), lambda bi, h, qi: (bi, 0, h)),
                  pl.BlockSpec((1, 2, ATT_TILE, ATT_TILE), lambda bi, h, qi: (h, 0, 0, 0)),
                  pl.BlockSpec(lam_vecs.shape, lambda bi, h, qi: (0, 0)),
                  pl.BlockSpec(subln_g.shape, lambda bi, h, qi: (0, 0))],
        out_specs=pl.BlockSpec((1, ATT_TILE, LANES), lambda bi, h, qi: (bi, qi, h)),
        out_shape=jax.ShapeDtypeStruct((b, s, A_HEADS * A_V_DIM), BF16),
        scratch_shapes=[pltpu.VMEM((2, ATT_TILE, 1), F32), pltpu.VMEM((2, ATT_TILE, 1), F32),
                        pltpu.VMEM((2, ATT_TILE, LANES), F32)],
        compiler_params=_cparams(("arbitrary", "arbitrary", "arbitrary")),
    )(cfar, qa, ka, va, bias, lam_vecs, subln_g)


def _attn_b_kernel(cfar_ref, qi_ref, ki_ref, wi_ref, qb_ref, kb_ref, vb_ref, bias_ref, o_ref,
                   keys_ref, cut_ref, m_ref, l_ref, acc_ref, *, seq_len, top_k):
    qt = pl.program_id(1)
    n_tiles = qt + 1
    t = ATT_TILE
    sub = t // LANES
    lo = _half_lane_mask()
    lane = lax.broadcasted_iota(jnp.int32, (1, LANES), 1)
    chunk_ok = _chunk_visible((t, t))

    def tile_start(j):
        return pl.multiple_of(j * t, t)

    w = wi_ref[0]

    def score_tile(j, carry):
        k0 = tile_start(j)
        k_lo, k_hi = _split_halves(ki_ref[0, pl.ds(k0, t), :], lo)
        score = None
        for pair in range(IDX_HEADS // 2):
            qp = qi_ref[0, :, pair * LANES:(pair + 1) * LANES]
            for half, kk in enumerate((k_lo, k_hi)):
                hh = 2 * pair + half
                logit = lax.dot_general(qp, kk, NT_DIMS, preferred_element_type=F32)
                term = w[:, hh:hh + 1] * jnp.maximum(logit, 0.0)
                score = term if score is None else score + term
        score = jnp.where(chunk_ok | (j < qt), score, -jnp.inf)
        bits = lax.bitcast_convert_type(score, jnp.int32)
        keys_ref[:, pl.ds(k0, t)] = jnp.where(bits < 0, (bits ^ 0x7FFFFFFF) + 1, bits)
        return carry

    lax.fori_loop(0, n_tiles, score_tile, 0)

    def count(pred_fn):
        def body(j, acc):
            k0 = tile_start(j)
            blk = keys_ref[:, pl.ds(k0, t)]
            for c in range(sub):
                idx = k0 + c * LANES + lane
                acc = acc + pred_fn(blk[:, c * LANES:(c + 1) * LANES], idx)
            return acc
        acc = lax.fori_loop(0, n_tiles, body, jnp.zeros((t, LANES), F32))
        return jnp.sum(acc, axis=-1, keepdims=True)

    kf = float(top_k)

    def value_step(i, thr):
        cand = thr + lax.shift_left(jnp.int32(1), jnp.asarray(31 - i, jnp.int32))
        cand_b = jnp.broadcast_to(cand, (t, LANES))
        cnt = count(lambda blk, idx: jnp.where(blk >= cand_b, 1.0, 0.0))
        return jnp.where(cnt >= kf, cand, thr)

    thr = lax.fori_loop(0, 32, value_step, jnp.full((t, 1), INT_MIN, jnp.int32))
    thr_b = jnp.broadcast_to(thr, (t, LANES))
    n_gt = count(lambda blk, idx: jnp.where(blk > thr_b, 1.0, 0.0))
    n_eq = count(lambda blk, idx: jnp.where(blk == thr_b, 1.0, 0.0))
    need = kf - n_gt
    cut_ref[...] = jnp.full(cut_ref.shape, seq_len, jnp.int32)
    any_tie = jnp.max(jnp.where(n_eq != need, 1.0, 0.0)) > 0.0

    @pl.when(any_tie)
    def _():
        n_bits = max(1, (seq_len - 1).bit_length())

        def index_step(i, c):
            cand = c + lax.shift_left(jnp.int32(1), jnp.asarray(n_bits - 1 - i, jnp.int32))
            cand_b = jnp.broadcast_to(cand, (t, LANES))
            below = count(lambda blk, idx: jnp.where(
                blk == thr_b, jnp.where(idx < cand_b, 1.0, 0.0), 0.0))
            return jnp.where(below < need, cand, c)

        c_star = lax.fori_loop(0, n_bits, index_step, jnp.zeros((t, 1), jnp.int32))
        cut_ref[...] = jnp.broadcast_to(c_star, cut_ref.shape)

    cut_b = cut_ref[...]

    def mask_tile(j, carry):
        k0 = tile_start(j)
        for c in range(sub):
            c0 = pl.multiple_of(k0 + c * LANES, LANES)
            blk = keys_ref[:, pl.ds(c0, LANES)]
            idx = c0 + lane
            tie_keep = jnp.where(idx <= cut_b, 0.0, NEG)
            mb = jnp.where(blk > thr_b, 0.0, jnp.where(blk == thr_b, tie_keep, NEG))
            keys_ref[:, pl.ds(c0, LANES)] = lax.bitcast_convert_type(mb, jnp.int32)
        return carry

    lax.fori_loop(0, n_tiles, mask_tile, 0)

    _init_flash(m_ref, l_ref, acc_ref)

    def attend(j, bias_idx):
        k0 = tile_start(j)
        k_lo, k_hi = _split_halves(kb_ref[0, pl.ds(k0, t), :], lo)
        vt = vb_ref[0, pl.ds(k0, t), :]
        mb = lax.bitcast_convert_type(keys_ref[:, pl.ds(k0, t)], F32)
        for hh in range(B_HEADS):
            qp = qb_ref[0, :, (hh // 2) * LANES:(hh // 2 + 1) * LANES]
            s = lax.dot_general(qp, k_lo if hh % 2 == 0 else k_hi, NT_DIMS,
                                preferred_element_type=F32) + mb
            if bias_idx is None:
                shift = cfar_ref[hh]
            else:
                s = s + bias_ref[hh, bias_idx]
                shift = 0.0
            _flash_update(s, vt, m_ref.at[hh], l_ref.at[hh], acc_ref.at[hh], shift)

    _key_tile_loop(qt, lambda j: attend(j, None), lambda j: attend(j, 1), lambda j: attend(j, 0))

    for pair in range(B_HEADS // 2):
        o_lo = acc_ref[2 * pair] / l_ref[2 * pair]
        o_hi = acc_ref[2 * pair + 1] / l_ref[2 * pair + 1]
        o_ref[0, :, pair * LANES:(pair + 1) * LANES] = jnp.where(lo, o_lo, o_hi).astype(o_ref.dtype)


def _attn_b(cfar, qi, ki2, wi, qb, kb2, vb2, bias, top_k):
    b, s, _ = qb.shape
    nq = s // ATT_TILE
    qmap = lambda bi, qt: (bi, qt, 0)
    kmap = lambda bi, qt: (bi, 0, 0)
    return pl.pallas_call(
        functools.partial(_attn_b_kernel, seq_len=s, top_k=top_k),
        grid=(b, nq),
        in_specs=[pl.BlockSpec(memory_space=pltpu.SMEM),
                  pl.BlockSpec((1, ATT_TILE, qi.shape[2]), qmap),
                  pl.BlockSpec((1, s, LANES), kmap),
                  pl.BlockSpec((1, ATT_TILE, LANES), qmap),
                  pl.BlockSpec((1, ATT_TILE, qb.shape[2]), qmap),
                  pl.BlockSpec((1, s, LANES), kmap),
                  pl.BlockSpec((1, s, LANES), kmap),
                  pl.BlockSpec(bias.shape, lambda bi, qt: (0, 0, 0, 0))],
        out_specs=pl.BlockSpec((1, ATT_TILE, B_HEADS * B_HEAD_DIM), qmap),
        out_shape=jax.ShapeDtypeStruct((b, s, B_HEADS * B_HEAD_DIM), BF16),
        scratch_shapes=[pltpu.VMEM((ATT_TILE, s), jnp.int32),
                        pltpu.VMEM((ATT_TILE, LANES), jnp.int32),
                        pltpu.VMEM((B_HEADS, ATT_TILE, 1), F32),
                        pltpu.VMEM((B_HEADS, ATT_TILE, 1), F32),
                        pltpu.VMEM((B_HEADS, ATT_TILE, LANES), F32)],
        compiler_params=_cparams(("arbitrary", "arbitrary")),
    )(cfar, qi, ki2, wi, qb, kb2, vb2, bias)


def _attn_c_kernel(q_ref, k_ref, v_ref, o_ref, m_ref, l_ref, acc_ref, *, scale):
    qt = pl.program_id(2)
    t = ATT_TILE
    lo = _half_lane_mask()
    chunk_ok = _chunk_visible((t, t))
    _init_flash(m_ref, l_ref, acc_ref)

    def tile(j, diagonal):
        k0 = pl.multiple_of(j * t, t)
        vt = v_ref[0, pl.ds(k0, t), :]
        for hh in range(2):
            q = q_ref[0, :, hh * LANES:(hh + 1) * LANES]
            kt = k_ref[0, pl.ds(k0, t), hh * LANES:(hh + 1) * LANES]
            s = lax.dot_general(q, kt, NT_DIMS, preferred_element_type=F32) * scale
            if diagonal:
                s = jnp.where(chunk_ok, s, NEG)
            _flash_update(s, vt, m_ref.at[hh], l_ref.at[hh], acc_ref.at[hh], 0.0)

    def far_body(j, carry):
        tile(j, False)
        return carry

    lax.fori_loop(0, qt, far_body, 0)
    tile(qt, True)
    o_ref[0] = jnp.where(lo, acc_ref[0] / l_ref[0], acc_ref[1] / l_ref[1]).astype(o_ref.dtype)


def _attn_c(q, k, v, scale):
    b, s, _ = q.shape
    nq = s // ATT_TILE
    pairs = C_HEADS // 2
    return pl.pallas_call(
        functools.partial(_attn_c_kernel, scale=scale),
        grid=(b, pairs, nq),
        in_specs=[pl.BlockSpec((1, ATT_TILE, 2 * LANES), lambda bi, p, qi: (bi, qi, p)),
                  pl.BlockSpec((1, s, 2 * LANES), lambda bi, p, qi: (bi, 0, p)),
                  pl.BlockSpec((1, s, LANES), lambda bi, p, qi: (bi, 0, p))],
        out_specs=pl.BlockSpec((1, ATT_TILE, LANES), lambda bi, p, qi: (bi, qi, p)),
        out_shape=jax.ShapeDtypeStruct((b, s, C_HEADS * C_V), BF16),
        scratch_shapes=[pltpu.VMEM((2, ATT_TILE, 1), F32), pltpu.VMEM((2, ATT_TILE, 1), F32),
                        pltpu.VMEM((2, ATT_TILE, LANES), F32)],
        compiler_params=_cparams(("arbitrary", "arbitrary", "arbitrary")),
    )(q, k, v)


def _dup(w):
    return jnp.concatenate([w, w], axis=1)


def _pad_cols(w, width):
    return jnp.pad(w, ((0, 0), (0, width - w.shape[1])))


def _rot_cols(w):
    half = w.shape[-1] // 2
    return jnp.concatenate([-w[..., half:], w[..., :half]], axis=-1)


def _rope_slot(w_rope):
    return jnp.pad(w_rope, ((0, 0), (0, 0), (C_NOPE, LANES - C_NOPE - C_ROPE)))


def kernel(x, rel_bias, e_norm, e_w_in, e_lam_q1, e_lam_k1, e_lam_q2, e_lam_k2, e_subln, e_w_o,
           o_norm, o_w_in, o_q_norm, o_w_uq, o_kv_norm, o_w_ukv, o_w_o, final_norm):
    b, s, d = x.shape
    assert s % ROW_TILE == 0 and s % ATT_TILE == 0 and ATT_TILE % CHUNK == 0
    assert ATT_TILE >= MAX_DISTANCE, "key tiles two or more before the diagonal must be all-far"
    top_k = min(TOPK_MAX, s // 4)
    assert top_k <= ATT_TILE, "every query must see at least top_k keys inside its own window"
    n = b * s
    x2 = x.reshape(n, d)

    bias = _bias_tiles(rel_bias)
    cfar = rel_bias[T5_FAR_BUCKET]

    w_in = e_w_in[0]
    sizes = (512, 512, 512, 512, 64, 64, 512, 64, 8, 1024)
    offs = np.concatenate([[0], np.cumsum(sizes)])
    col = lambda i: w_in[:, offs[i]:offs[i + 1]]
    qk_scale = A_HEAD_DIM ** -0.5
    w_main = jnp.concatenate([col(0) * qk_scale, col(1), col(2), col(3) * (B_HEAD_DIM ** -0.5),
                              col(6)], axis=1).astype(BF16)
    w_small = jnp.concatenate([_dup(col(4)), _dup(col(5)), _dup(col(7)), _pad_cols(col(8), LANES)],
                              axis=1).astype(BF16)
    w_gate0 = col(9).astype(BF16)
    qa, ka, va, qb, qi, kb2, vb2, ki2, wi, gate0 = _proj0(
        x2, e_norm[0][None], w_main, w_small, w_gate0)
    r3 = lambda a: a.reshape(b, s, a.shape[1])

    lam_init = 0.8 - 0.6 * math.exp(-0.3 * 0)
    lam_vecs = jnp.stack([e_lam_q1[0], e_lam_k1[0], e_lam_q2[0], e_lam_k2[0]])
    o_a = _attn_a(cfar[:A_HEADS], r3(qa), r3(ka), r3(va), bias[:A_HEADS], lam_vecs,
                  e_subln[0][None], lam_init)
    o_b = _attn_b(cfar[A_HEADS:], r3(qi), r3(ki2), r3(wi), r3(qb), r3(kb2), r3(vb2),
                  bias[A_HEADS:], top_k)
    h1 = _out_proj([o_a.reshape(n, -1), o_b.reshape(n, -1)], gate0, x2, e_w_o[0].astype(BF16))

    w_in1 = o_w_in[0]
    w_ql = w_in1[:, :Q_LORA]
    w_kvl = w_in1[:, Q_LORA:Q_LORA + KV_LORA]
    w_kr = w_in1[:, Q_LORA + KV_LORA:Q_LORA + KV_LORA + C_ROPE]
    w_gate1 = w_in1[:, Q_LORA + KV_LORA + C_ROPE:].astype(BF16)
    kr_slot = _rope_slot(w_kr[:, None, :])[:, 0]
    kr_rot_slot = _rope_slot(_rot_cols(w_kr)[:, None, :])[:, 0]
    w_lat = jnp.concatenate([w_ql, w_kvl, kr_slot, kr_rot_slot], axis=1).astype(BF16)

    w_uq = o_w_uq[0].reshape(Q_LORA, C_HEADS, C_NOPE + C_ROPE)
    w_q = jnp.concatenate([w_uq[..., :C_NOPE], jnp.zeros((Q_LORA, C_HEADS, LANES - C_NOPE), F32)],
                          axis=-1) + _rope_slot(w_uq[..., C_NOPE:])
    w_qrot = _rope_slot(_rot_cols(w_uq[..., C_NOPE:]))
    w_ukv = o_w_ukv[0].reshape(KV_LORA, C_HEADS, C_NOPE + C_V)
    w_k = jnp.pad(w_ukv[..., :C_NOPE], ((0, 0), (0, 0), (0, LANES - C_NOPE)))
    w_v = w_ukv[..., C_NOPE:]
    flat = lambda w: w.reshape(w.shape[0], -1).astype(BF16)

    inv = ROPE_THETA ** (-jnp.arange(0, C_ROPE, 2, dtype=F32) / C_ROPE)
    ang = jnp.arange(s, dtype=F32)[:, None] * inv[None, :]
    cos, sin = jnp.cos(ang), jnp.sin(ang)
    pad_hi = jnp.zeros((s, LANES - C_NOPE - C_ROPE), F32)
    cos_t = jnp.concatenate([jnp.ones((s, C_NOPE), F32), cos, cos, pad_hi], axis=1)
    sin_t = jnp.concatenate([jnp.zeros((s, C_NOPE), F32), sin, sin, pad_hi], axis=1)

    q1, k1, v1, gate1 = _proj1(h1, o_norm[0][None], cos_t, sin_t, w_lat, w_gate1,
                               o_q_norm[0][None], o_kv_norm[0][None],
                               flat(w_q), flat(w_qrot), flat(w_k), flat(w_v))
    o_c = _attn_c(r3(q1), r3(k1), r3(v1), (C_NOPE + C_ROPE) ** -0.5)
    out = _out_proj([o_c.reshape(n, -1)], gate1, h1, o_w_o[0].astype(BF16), final_norm[None])
    return out.reshape(b, s, d)
```

```python
import functools
import math

import numpy as np
import jax
import jax.numpy as jnp
from jax import lax
from jax.experimental import pallas as pl
from jax.experimental.pallas import tpu as pltpu

F32 = jnp.float32
BF16 = jnp.bfloat16

CHUNK = 64
RMS_EPS = 1e-6
N_BUCKETS = 32
MAX_DISTANCE = 128
A_HEADS = 4
A_HEAD_DIM = 64
A_V_DIM = 2 * A_HEAD_DIM
B_HEADS = 8
B_HEAD_DIM = 64
IDX_HEADS = 8
IDX_DIM = 64
TOPK_MAX = 256
C_HEADS = 16
C_NOPE = 64
C_ROPE = 32
C_V = 64
Q_LORA = 384
KV_LORA = 256
ROPE_THETA = 10000.0

LANES = 128
ATT_TILE = 256
ROW_TILE = 512
VMEM_LIMIT = 48 * 1024 * 1024

LOG2E = math.log2(math.e)
NEG = -1e30
INT_MIN = -(2 ** 31)
NT_DIMS = (((1,), (1,)), ((), ()))


def _t5_thresholds():
    nb = N_BUCKETS // 2
    max_exact = nb // 2
    n = np.arange(max_exact, MAX_DISTANCE + 1, dtype=np.float64)
    large = max_exact + (np.log(n / max_exact) / math.log(MAX_DISTANCE / max_exact)
                         * (nb - max_exact)).astype(np.int64)
    large = np.minimum(large, nb - 1)
    thr = [int(n[np.argmax(large >= b)]) for b in range(max_exact + 1, nb)]
    return max_exact, thr


T5_MAX_EXACT, T5_THRESHOLDS = _t5_thresholds()
T5_FAR_BUCKET = N_BUCKETS // 2 - 1


def _cparams(sem):
    return pltpu.CompilerParams(dimension_semantics=sem, vmem_limit_bytes=VMEM_LIMIT)


def _rms(x, g):
    return x * lax.rsqrt(jnp.mean(x * x, axis=-1, keepdims=True) + RMS_EPS) * g


def _silu(g):
    return g / (1.0 + jnp.exp(-g))


def _chunk_visible(shape):
    shift = CHUNK.bit_length() - 1
    assert 1 << shift == CHUNK
    row = lax.broadcasted_iota(jnp.int32, shape, 0)
    col = lax.broadcasted_iota(jnp.int32, shape, 1)
    return (col >> shift) <= (row >> shift)


def _bias_tile_kernel(tab_ref, o_ref):
    h = pl.program_id(0)
    t = pl.program_id(1)
    shape = (ATT_TILE, ATT_TILE)
    row = lax.broadcasted_iota(jnp.int32, shape, 0)
    col = lax.broadcasted_iota(jnp.int32, shape, 1)
    rel = col - row - t * ATT_TILE
    n = jnp.abs(rel)
    large = jnp.full(shape, T5_MAX_EXACT, jnp.int32)
    for thr in T5_THRESHOLDS:
        large = large + jnp.where(n >= thr, 1, 0)
    bucket = jnp.where(rel > 0, N_BUCKETS // 2, 0) + jnp.where(n < T5_MAX_EXACT, n, large)
    val = jnp.full(shape, tab_ref[0, h], F32)
    for b in range(1, N_BUCKETS):
        val = jnp.where(bucket == b, tab_ref[b, h], val)
    visible = _chunk_visible(shape) | (t > 0)
    o_ref[0, 0] = jnp.where(visible, val, NEG)


def _bias_tiles(rel_bias):
    heads = rel_bias.shape[1]
    return pl.pallas_call(
        _bias_tile_kernel,
        grid=(heads, 2),
        in_specs=[pl.BlockSpec(memory_space=pltpu.SMEM)],
        out_specs=pl.BlockSpec((1, 1, ATT_TILE, ATT_TILE), lambda h, t: (h, t, 0, 0)),
        out_shape=jax.ShapeDtypeStruct((heads, 2, ATT_TILE, ATT_TILE), F32),
        compiler_params=_cparams(("arbitrary", "arbitrary")),
    )(rel_bias)


def _proj0_kernel(x_ref, g_ref, w_main_ref, w_small_ref, w_gate_ref,
                  qa_ref, ka_ref, va_ref, qb_ref, qi_ref, kb_ref, vb_ref, ki_ref, wi_ref, gate_ref):
    hn = _rms(x_ref[...], g_ref[...]).astype(BF16)
    main = (qa_ref, ka_ref, va_ref, qb_ref, qi_ref)
    for i, o_ref in enumerate(main):
        w = w_main_ref[:, i * 512:(i + 1) * 512]
        o_ref[...] = jnp.dot(hn, w, preferred_element_type=F32).astype(o_ref.dtype)
    small = (kb_ref, vb_ref, ki_ref, wi_ref)
    for i, o_ref in enumerate(small):
        w = w_small_ref[:, i * LANES:(i + 1) * LANES]
        o_ref[...] = jnp.dot(hn, w, preferred_element_type=F32).astype(o_ref.dtype)
    gate_ref[...] = jnp.dot(hn, w_gate_ref[...], preferred_element_type=F32).astype(gate_ref.dtype)


def _proj0(x2, g, w_main, w_small, w_gate):
    n, d = x2.shape
    rows = lambda i: (i, 0)
    const = lambda i: (0, 0)
    outs = [(512, BF16)] * 5 + [(LANES, BF16)] * 3 + [(LANES, F32), (w_gate.shape[1], BF16)]
    return pl.pallas_call(
        _proj0_kernel,
        grid=(n // ROW_TILE,),
        in_specs=[pl.BlockSpec((ROW_TILE, d), rows), pl.BlockSpec((1, d), const),
                  pl.BlockSpec(w_main.shape, const), pl.BlockSpec(w_small.shape, const),
                  pl.BlockSpec(w_gate.shape, const)],
        out_specs=[pl.BlockSpec((ROW_TILE, c), rows) for c, _ in outs],
        out_shape=[jax.ShapeDtypeStruct((n, c), dt) for c, dt in outs],
        compiler_params=_cparams(("arbitrary",)),
    )(x2, g, w_main, w_small, w_gate)


def _proj1_kernel(x_ref, g_ref, cos_ref, sin_ref, w_lat_ref, w_gate_ref, qn_ref, kvn_ref,
                  w_q_ref, w_qrot_ref, w_k_ref, w_v_ref, q_ref, k_ref, v_ref, gate_ref):
    hn = _rms(x_ref[...], g_ref[...]).astype(BF16)
    cos = cos_ref[...]
    sin = sin_ref[...]
    lat = jnp.dot(hn, w_lat_ref[...], preferred_element_type=F32)
    q_lat = lat[:, :Q_LORA]
    kv_lat = lat[:, Q_LORA:Q_LORA + KV_LORA]
    kr = lat[:, Q_LORA + KV_LORA:Q_LORA + KV_LORA + LANES]
    kr_rot = lat[:, Q_LORA + KV_LORA + LANES:]
    k_rope = kr * cos + kr_rot * sin
    qn = _rms(q_lat, qn_ref[...]).astype(BF16)
    kvn = _rms(kv_lat, kvn_ref[...]).astype(BF16)
    for h in range(C_HEADS):
        sl = slice(h * LANES, (h + 1) * LANES)
        qa = jnp.dot(qn, w_q_ref[:, sl], preferred_element_type=F32)
        qr = jnp.dot(qn, w_qrot_ref[:, sl], preferred_element_type=F32)
        q_ref[:, sl] = (qa * cos + qr * sin).astype(q_ref.dtype)
        kn = jnp.dot(kvn, w_k_ref[:, sl], preferred_element_type=F32)
        k_ref[:, sl] = (kn + k_rope).astype(k_ref.dtype)
    v_ref[...] = jnp.dot(kvn, w_v_ref[...], preferred_element_type=F32).astype(v_ref.dtype)
    gate_ref[...] = jnp.dot(hn, w_gate_ref[...], preferred_element_type=F32).astype(gate_ref.dtype)


def _proj1(x2, g, cos_t, sin_t, w_lat, w_gate, qn_g, kvn_g, w_q, w_qrot, w_k, w_v):
    n, d = x2.shape
    seq_tiles = cos_t.shape[0] // ROW_TILE
    rows = lambda i: (i, 0)
    const = lambda i: (0, 0)
    pos = lambda i: (i % seq_tiles, 0)
    full = lambda a: pl.BlockSpec(a.shape, const)
    outs = [(C_HEADS * LANES, BF16), (C_HEADS * LANES, BF16), (C_HEADS * C_V, BF16),
            (w_gate.shape[1], BF16)]
    return pl.pallas_call(
        _proj1_kernel,
        grid=(n // ROW_TILE,),
        in_specs=[pl.BlockSpec((ROW_TILE, d), rows), full(g),
                  pl.BlockSpec((ROW_TILE, LANES), pos), pl.BlockSpec((ROW_TILE, LANES), pos),
                  full(w_lat), full(w_gate), full(qn_g), full(kvn_g),
                  full(w_q), full(w_qrot), full(w_k), full(w_v)],
        out_specs=[pl.BlockSpec((ROW_TILE, c), rows) for c, _ in outs],
        out_shape=[jax.ShapeDtypeStruct((n, c), dt) for c, dt in outs],
        compiler_params=_cparams(("arbitrary",)),
    )(x2, g, cos_t, sin_t, w_lat, w_gate, qn_g, kvn_g, w_q, w_qrot, w_k, w_v)


def _out_kernel(*refs, n_mix, final_norm):
    mix_refs = refs[:n_mix]
    gate_ref, x_ref, w_ref = refs[n_mix:n_mix + 3]
    rest = refs[n_mix + 3:]
    sg = _silu(gate_ref[...].astype(F32))
    y = x_ref[...]
    off = 0
    for m_ref in mix_refs:
        c = m_ref.shape[1]
        a = (m_ref[...].astype(F32) * sg[:, off:off + c]).astype(BF16)
        y = y + jnp.dot(a, w_ref[off:off + c, :], preferred_element_type=F32)
        off += c
    if final_norm:
        g_ref, o_ref = rest
        o_ref[...] = _rms(y, g_ref[...])
    else:
        (o_ref,) = rest
        o_ref[...] = y


def _out_proj(mixes, gate, x2, w_o, final_g=None):
    n, d = x2.shape
    rows = lambda i: (i, 0)
    const = lambda i: (0, 0)
    in_specs = [pl.BlockSpec((ROW_TILE, m.shape[1]), rows) for m in mixes]
    in_specs += [pl.BlockSpec((ROW_TILE, gate.shape[1]), rows), pl.BlockSpec((ROW_TILE, d), rows),
                 pl.BlockSpec(w_o.shape, const)]
    args = list(mixes) + [gate, x2, w_o]
    if final_g is not None:
        in_specs.append(pl.BlockSpec((1, d), const))
        args.append(final_g)
    return pl.pallas_call(
        functools.partial(_out_kernel, n_mix=len(mixes), final_norm=final_g is not None),
        grid=(n // ROW_TILE,),
        in_specs=in_specs,
        out_specs=pl.BlockSpec((ROW_TILE, d), rows),
        out_shape=jax.ShapeDtypeStruct((n, d), F32),
        compiler_params=_cparams(("arbitrary",)),
    )(*args)


def _flash_update(s, v, m_ref, l_ref, acc_ref, shift):
    m_old = m_ref[...]
    m_new = jnp.maximum(m_old, jnp.max(s, axis=-1, keepdims=True) + shift)
    p = jnp.exp(s - (m_new - shift))
    alpha = jnp.exp(m_old - m_new)
    l_ref[...] = alpha * l_ref[...] + jnp.sum(p, axis=-1, keepdims=True)
    acc_ref[...] = alpha * acc_ref[...] + jnp.dot(p.astype(v.dtype), v, preferred_element_type=F32)
    m_ref[...] = m_new


def _init_flash(m_ref, l_ref, acc_ref):
    m_ref[...] = jnp.full(m_ref.shape, -jnp.inf, F32)
    l_ref[...] = jnp.zeros(l_ref.shape, F32)
    acc_ref[...] = jnp.zeros(acc_ref.shape, F32)


def _half_lane_mask():
    return lax.broadcasted_iota(jnp.int32, (1, LANES), 1) < (LANES // 2)


def _split_halves(kt, lo):
    zero = jnp.zeros_like(kt)
    return jnp.where(lo, kt, zero), jnp.where(lo, zero, kt)


def _key_tile_loop(qt, far_fn, near_fn, diag_fn):
    def far_body(j, carry):
        far_fn(j)
        return carry
    lax.fori_loop(0, jnp.maximum(qt - 1, 0), far_body, 0)

    @pl.when(qt >= 1)
    def _():
        near_fn(qt - 1)

    diag_fn(qt)


def _attn_a_kernel(cfar_ref, q_ref, k_ref, v_ref, bias_ref, lam_ref, g_ref, o_ref,
                   m_ref, l_ref, acc_ref, *, lam_init):
    h = pl.program_id(1)
    qt = pl.program_id(2)
    lo = _half_lane_mask()
    q = q_ref[0]
    _init_flash(m_ref, l_ref, acc_ref)

    def tile(j, bias_idx):
        k0 = pl.multiple_of(j * ATT_TILE, ATT_TILE)
        k1, k2 = _split_halves(k_ref[0, pl.ds(k0, ATT_TILE), :], lo)
        vt = v_ref[0, pl.ds(k0, ATT_TILE), :]
        for mp, kk in enumerate((k1, k2)):
            s = lax.dot_general(q, kk, NT_DIMS, preferred_element_type=F32)
            if bias_idx is None:
                shift = cfar_ref[h]
            else:
                s = s + bias_ref[0, bias_idx]
                shift = 0.0
            _flash_update(s, vt, m_ref.at[mp], l_ref.at[mp], acc_ref.at[mp], shift)

    _key_tile_loop(qt, lambda j: tile(j, None), lambda j: tile(j, 1), lambda j: tile(j, 0))

    lam_v = lam_ref[...]
    lam = (jnp.exp(jnp.sum(lam_v[0:1] * lam_v[1:2], axis=-1, keepdims=True))
           - jnp.exp(jnp.sum(lam_v[2:3] * lam_v[3:4], axis=-1, keepdims=True)) + lam_init)
    o = acc_ref[0] / l_ref[0] - lam * (acc_ref[1] / l_ref[1])
    o_ref[0] = (_rms(o, g_ref[...]) * (1.0 - lam_init)).astype(o_ref.dtype)


def _attn_a(cfar, qa, ka, va, bias, lam_vecs, subln_g, lam_init):
    b, s, _ = qa.shape
    nq = s // ATT_TILE
    return pl.pallas_call(
        functools.partial(_attn_a_kernel, lam_init=lam_init),
        grid=(b, A_HEADS, nq),
        in_specs=[pl.BlockSpec(memory_space=pltpu.SMEM),
                  pl.BlockSpec((1, ATT_TILE, LANES), lambda bi, h, qi: (bi, qi, h)),
                  pl.BlockSpec((1, s, LANES), lambda bi, h, qi: (bi, 0, h)),
                  pl.BlockSpec((1, s, LANES), lambda bi, h, qi: (bi, 0, h)),
                  pl.BlockSpec((1, 2, ATT_TILE, ATT_TILE), lambda bi, h, qi: (h, 0, 0, 0)),
                  pl.BlockSpec(lam_vecs.shape, lambda bi, h, qi: (0, 0)),
                  pl.BlockSpec(subln_g.shape, lambda bi, h, qi: (0, 0))],
        out_specs=pl.BlockSpec((1, ATT_TILE, LANES), lambda bi, h, qi: (bi, qi, h)),
        out_shape=jax.ShapeDtypeStruct((b, s, A_HEADS * A_V_DIM), BF16),
        scratch_shapes=[pltpu.VMEM((2, ATT_TILE, 1), F32), pltpu.VMEM((2, ATT_TILE, 1), F32),
                        pltpu.VMEM((2, ATT_TILE, LANES), F32)],
        compiler_params=_cparams(("arbitrary", "arbitrary", "arbitrary")),
    )(cfar, qa, ka, va, bias, lam_vecs, subln_g)


def _attn_b_kernel(cfar_ref, qi_ref, ki_ref, wi_ref, qb_ref, kb_ref, vb_ref, bias_ref, o_ref,
                   keys_ref, cut_ref, m_ref, l_ref, acc_ref, *, seq_len, top_k):
    qt = pl.program_id(1)
    n_tiles = qt + 1
    t = ATT_TILE
    sub = t // LANES
    lo = _half_lane_mask()
    lane = lax.broadcasted_iota(jnp.int32, (1, LANES), 1)
    chunk_ok = _chunk_visible((t, t))

    def tile_start(j):
        return pl.multiple_of(j * t, t)

    w = wi_ref[0]

    def score_tile(j, carry):
        k0 = tile_start(j)
        k_lo, k_hi = _split_halves(ki_ref[0, pl.ds(k0, t), :], lo)
        score = None
        for pair in range(IDX_HEADS // 2):
            qp = qi_ref[0, :, pair * LANES:(pair + 1) * LANES]
            for half, kk in enumerate((k_lo, k_hi)):
                hh = 2 * pair + half
                logit = lax.dot_general(qp, kk, NT_DIMS, preferred_element_type=F32)
                term = w[:, hh:hh + 1] * jnp.maximum(logit, 0.0)
                score = term if score is None else score + term
        score = jnp.where(chunk_ok | (j < qt), score, -jnp.inf)
        bits = lax.bitcast_convert_type(score, jnp.int32)
        keys_ref[:, pl.ds(k0, t)] = jnp.where(bits < 0, (bits ^ 0x7FFFFFFF) + 1, bits)
        return carry

    lax.fori_loop(0, n_tiles, score_tile, 0)

    def count(pred_fn):
        def body(j, acc):
            k0 = tile_start(j)
            blk = keys_ref[:, pl.ds(k0, t)]
            for c in range(sub):
                idx = k0 + c * LANES + lane
                acc = acc + pred_fn(blk[:, c * LANES:(c + 1) * LANES], idx)
            return acc
        acc = lax.fori_loop(0, n_tiles, body, jnp.zeros((t, LANES), F32))
        return jnp.sum(acc, axis=-1, keepdims=True)

    kf = float(top_k)

    def value_step(i, thr):
        cand = thr + lax.shift_left(jnp.int32(1), jnp.asarray(31 - i, jnp.int32))
        cand_b = jnp.broadcast_to(cand, (t, LANES))
        cnt = count(lambda blk, idx: jnp.where(blk >= cand_b, 1.0, 0.0))
        return jnp.where(cnt >= kf, cand, thr)

    thr = lax.fori_loop(0, 32, value_step, jnp.full((t, 1), INT_MIN, jnp.int32))
    thr_b = jnp.broadcast_to(thr, (t, LANES))
    n_gt = count(lambda blk, idx: jnp.where(blk > thr_b, 1.0, 0.0))
    n_eq = count(lambda blk, idx: jnp.where(blk == thr_b, 1.0, 0.0))
    need = kf - n_gt
    cut_ref[...] = jnp.full(cut_ref.shape, seq_len, jnp.int32)
    any_tie = jnp.max(jnp.where(n_eq != need, 1.0, 0.0)) > 0.0

    @pl.when(any_tie)
    def _():
        n_bits = max(1, (seq_len - 1).bit_length())

        def index_step(i, c):
            cand = c + lax.shift_left(jnp.int32(1), jnp.asarray(n_bits - 1 - i, jnp.int32))
            cand_b = jnp.broadcast_to(cand, (t, LANES))
            below = count(lambda blk, idx: jnp.where(
                blk == thr_b, jnp.where(idx < cand_b, 1.0, 0.0), 0.0))
            return jnp.where(below < need, cand, c)

        c_star = lax.fori_loop(0, n_bits, index_step, jnp.zeros((t, 1), jnp.int32))
        cut_ref[...] = jnp.broadcast_to(c_star, cut_ref.shape)

    cut_b = cut_ref[...]

    def mask_tile(j, carry):
        k0 = tile_start(j)
        for c in range(sub):
            c0 = pl.multiple_of(k0 + c * LANES, LANES)
            blk = keys_ref[:, pl.ds(c0, LANES)]
            idx = c0 + lane
            tie_keep = jnp.where(idx <= cut_b, 0.0, NEG)
            mb = jnp.where(blk > thr_b, 0.0, jnp.where(blk == thr_b, tie_keep, NEG))
            keys_ref[:, pl.ds(c0, LANES)] = lax.bitcast_convert_type(mb, jnp.int32)
        return carry

    lax.fori_loop(0, n_tiles, mask_tile, 0)

    _init_flash(m_ref, l_ref, acc_ref)

    def attend(j, bias_idx):
        k0 = tile_start(j)
        k_lo, k_hi = _split_halves(kb_ref[0, pl.ds(k0, t), :], lo)
        vt = vb_ref[0, pl.ds(k0, t), :]
        mb = lax.bitcast_convert_type(keys_ref[:, pl.ds(k0, t)], F32)
        for hh in range(B_HEADS):
            qp = qb_ref[0, :, (hh // 2) * LANES:(hh // 2 + 1) * LANES]
            s = lax.dot_general(qp, k_lo if hh % 2 == 0 else k_hi, NT_DIMS,
                                preferred_element_type=F32) + mb
            if bias_idx is None:
                shift = cfar_ref[hh]
            else:
                s = s + bias_ref[hh, bias_idx]
                shift = 0.0
            _flash_update(s, vt, m_ref.at[hh], l_ref.at[hh], acc_ref.at[hh], shift)

    _key_tile_loop(qt, lambda j: attend(j, None), lambda j: attend(j, 1), lambda j: attend(j, 0))

    for pair in range(B_HEADS // 2):
        o_lo = acc_ref[2 * pair] / l_ref[2 * pair]
        o_hi = acc_ref[2 * pair + 1] / l_ref[2 * pair + 1]
        o_ref[0, :, pair * LANES:(pair + 1) * LANES] = jnp.where(lo, o_lo, o_hi).astype(o_ref.dtype)


def _attn_b(cfar, qi, ki2, wi, qb, kb2, vb2, bias, top_k):
    b, s, _ = qb.shape
    nq = s // ATT_TILE
    qmap = lambda bi, qt: (bi, qt, 0)
    kmap = lambda bi, qt: (bi, 0, 0)
    return pl.pallas_call(
        functools.partial(_attn_b_kernel, seq_len=s, top_k=top_k),
        grid=(b, nq),
        in_specs=[pl.BlockSpec(memory_space=pltpu.SMEM),
                  pl.BlockSpec((1, ATT_TILE, qi.shape[2]), qmap),
                  pl.BlockSpec((1, s, LANES), kmap),
                  pl.BlockSpec((1, ATT_TILE, LANES), qmap),
                  pl.BlockSpec((1, ATT_TILE, qb.shape[2]), qmap),
                  pl.BlockSpec((1, s, LANES), kmap),
                  pl.BlockSpec((1, s, LANES), kmap),
                  pl.BlockSpec(bias.shape, lambda bi, qt: (0, 0, 0, 0))],
        out_specs=pl.BlockSpec((1, ATT_TILE, B_HEADS * B_HEAD_DIM), qmap),
        out_shape=jax.ShapeDtypeStruct((b, s, B_HEADS * B_HEAD_DIM), BF16),
        scratch_shapes=[pltpu.VMEM((ATT_TILE, s), jnp.int32),
                        pltpu.VMEM((ATT_TILE, LANES), jnp.int32),
                        pltpu.VMEM((B_HEADS, ATT_TILE, 1), F32),
                        pltpu.VMEM((B_HEADS, ATT_TILE, 1), F32),
                        pltpu.VMEM((B_HEADS, ATT_TILE, LANES), F32)],
        compiler_params=_cparams(("arbitrary", "arbitrary")),
    )(cfar, qi, ki2, wi, qb, kb2, vb2, bias)


C_GROUP = 8


def _flash_update_t(s_t, v_t, m_ref, l_ref, acc_ref):
    m_old = m_ref[...]
    m_new = jnp.maximum(m_old, jnp.max(s_t, axis=0, keepdims=True))
    p = jnp.exp2(s_t - m_new)
    alpha = jnp.exp2(m_old - m_new)
    l_ref[...] = alpha * l_ref[...] + jnp.sum(p, axis=0, keepdims=True)
    acc_ref[...] = alpha * acc_ref[...] + jnp.dot(v_t, p.astype(v_t.dtype),
                                                  preferred_element_type=F32)
    m_ref[...] = m_new


def _attn_c_kernel(qt_ref, k_ref, vt_ref, o_ref, m_ref, l_ref, acc_ref, *, scale):
    qt = pl.program_id(2)
    t = ATT_TILE
    key = lax.broadcasted_iota(jnp.int32, (t, t), 0)
    qry = lax.broadcasted_iota(jnp.int32, (t, t), 1)
    shift = CHUNK.bit_length() - 1
    chunk_ok = (key >> shift) <= (qry >> shift)
    _init_flash(m_ref, l_ref, acc_ref)

    def tile(j, diagonal):
        k0 = pl.multiple_of(j * t, t)
        scores = []
        for g in range(C_GROUP):
            kt = k_ref[0, pl.ds(k0, t), g * LANES:(g + 1) * LANES]
            scores.append(jnp.dot(kt, qt_ref[0, g * LANES:(g + 1) * LANES, :],
                                  preferred_element_type=F32))
        for g in range(C_GROUP):
            s_t = scores[g] * (scale * LOG2E)
            if diagonal:
                s_t = jnp.where(chunk_ok, s_t, NEG)
            v_t = vt_ref[0, g * C_V:(g + 1) * C_V, pl.ds(k0, t)]
            _flash_update_t(s_t, v_t, m_ref.at[g], l_ref.at[g], acc_ref.at[g])

    def far_body(j, carry):
        tile(j, False)
        return carry

    lax.fori_loop(0, qt, far_body, 0)
    tile(qt, True)
    o_t = jnp.concatenate([acc_ref[g] / l_ref[g] for g in range(C_GROUP)], axis=0)
    o_ref[0] = o_t.T.astype(o_ref.dtype)


def _attn_c(q_t, k, v_t, scale):
    b, s, _ = k.shape
    nq = s // ATT_TILE
    groups = C_HEADS // C_GROUP
    return pl.pallas_call(
        functools.partial(_attn_c_kernel, scale=scale),
        grid=(b, groups, nq),
        in_specs=[pl.BlockSpec((1, C_GROUP * LANES, ATT_TILE), lambda bi, p, qi: (bi, p, qi)),
                  pl.BlockSpec((1, s, C_GROUP * LANES), lambda bi, p, qi: (bi, 0, p)),
                  pl.BlockSpec((1, C_GROUP * C_V, s), lambda bi, p, qi: (bi, p, 0))],
        out_specs=pl.BlockSpec((1, ATT_TILE, C_GROUP * C_V), lambda bi, p, qi: (bi, qi, p)),
        out_shape=jax.ShapeDtypeStruct((b, s, C_HEADS * C_V), BF16),
        scratch_shapes=[pltpu.VMEM((C_GROUP, 1, ATT_TILE), F32),
                        pltpu.VMEM((C_GROUP, 1, ATT_TILE), F32),
                        pltpu.VMEM((C_GROUP, C_V, ATT_TILE), F32)],
        compiler_params=_cparams(("arbitrary", "arbitrary", "arbitrary")),
    )(q_t, k, v_t)


def _dup(w):
    return jnp.concatenate([w, w], axis=1)


def _pad_cols(w, width):
    return jnp.pad(w, ((0, 0), (0, width - w.shape[1])))


def _rot_cols(w):
    half = w.shape[-1] // 2
    return jnp.concatenate([-w[..., half:], w[..., :half]], axis=-1)


def _rope_slot(w_rope):
    return jnp.pad(w_rope, ((0, 0), (0, 0), (C_NOPE, LANES - C_NOPE - C_ROPE)))


def kernel(x, rel_bias, e_norm, e_w_in, e_lam_q1, e_lam_k1, e_lam_q2, e_lam_k2, e_subln, e_w_o,
           o_norm, o_w_in, o_q_norm, o_w_uq, o_kv_norm, o_w_ukv, o_w_o, final_norm):
    b, s, d = x.shape
    assert s % ROW_TILE == 0 and s % ATT_TILE == 0 and ATT_TILE % CHUNK == 0
    assert ATT_TILE >= MAX_DISTANCE, "key tiles two or more before the diagonal must be all-far"
    top_k = min(TOPK_MAX, s // 4)
    assert top_k <= ATT_TILE, "every query must see at least top_k keys inside its own window"
    n = b * s
    x2 = x.reshape(n, d)

    bias = _bias_tiles(rel_bias)
    cfar = rel_bias[T5_FAR_BUCKET]

    w_in = e_w_in[0]
    sizes = (512, 512, 512, 512, 64, 64, 512, 64, 8, 1024)
    offs = np.concatenate([[0], np.cumsum(sizes)])
    col = lambda i: w_in[:, offs[i]:offs[i + 1]]
    qk_scale = A_HEAD_DIM ** -0.5
    w_main = jnp.concatenate([col(0) * qk_scale, col(1), col(2), col(3) * (B_HEAD_DIM ** -0.5),
                              col(6)], axis=1).astype(BF16)
    w_small = jnp.concatenate([_dup(col(4)), _dup(col(5)), _dup(col(7)), _pad_cols(col(8), LANES)],
                              axis=1).astype(BF16)
    w_gate0 = col(9).astype(BF16)
    qa, ka, va, qb, qi, kb2, vb2, ki2, wi, gate0 = _proj0(
        x2, e_norm[0][None], w_main, w_small, w_gate0)
    r3 = lambda a: a.reshape(b, s, a.shape[1])

    lam_init = 0.8 - 0.6 * math.exp(-0.3 * 0)
    lam_vecs = jnp.stack([e_lam_q1[0], e_lam_k1[0], e_lam_q2[0], e_lam_k2[0]])
    o_a = _attn_a(cfar[:A_HEADS], r3(qa), r3(ka), r3(va), bias[:A_HEADS], lam_vecs,
                  e_subln[0][None], lam_init)
    o_b = _attn_b(cfar[A_HEADS:], r3(qi), r3(ki2), r3(wi), r3(qb), r3(kb2), r3(vb2),
                  bias[A_HEADS:], top_k)
    h1 = _out_proj([o_a.reshape(n, -1), o_b.reshape(n, -1)], gate0, x2, e_w_o[0].astype(BF16))

    w_in1 = o_w_in[0]
    w_ql = w_in1[:, :Q_LORA]
    w_kvl = w_in1[:, Q_LORA:Q_LORA + KV_LORA]
    w_kr = w_in1[:, Q_LORA + KV_LORA:Q_LORA + KV_LORA + C_ROPE]
    w_gate1 = w_in1[:, Q_LORA + KV_LORA + C_ROPE:].astype(BF16)
    kr_slot = _rope_slot(w_kr[:, None, :])[:, 0]
    kr_rot_slot = _rope_slot(_rot_cols(w_kr)[:, None, :])[:, 0]
    w_lat = jnp.concatenate([w_ql, w_kvl, kr_slot, kr_rot_slot], axis=1).astype(BF16)

    w_uq = o_w_uq[0].reshape(Q_LORA, C_HEADS, C_NOPE + C_ROPE)
    w_q = jnp.concatenate([w_uq[..., :C_NOPE], jnp.zeros((Q_LORA, C_HEADS, LANES - C_NOPE), F32)],
                          axis=-1) + _rope_slot(w_uq[..., C_NOPE:])
    w_qrot = _rope_slot(_rot_cols(w_uq[..., C_NOPE:]))
    w_ukv = o_w_ukv[0].reshape(KV_LORA, C_HEADS, C_NOPE + C_V)
    w_k = jnp.pad(w_ukv[..., :C_NOPE], ((0, 0), (0, 0), (0, LANES - C_NOPE)))
    w_v = w_ukv[..., C_NOPE:]
    flat = lambda w: w.reshape(w.shape[0], -1).astype(BF16)

    inv = ROPE_THETA ** (-jnp.arange(0, C_ROPE, 2, dtype=F32) / C_ROPE)
    ang = jnp.arange(s, dtype=F32)[:, None] * inv[None, :]
    cos, sin = jnp.cos(ang), jnp.sin(ang)
    pad_hi = jnp.zeros((s, LANES - C_NOPE - C_ROPE), F32)
    cos_t = jnp.concatenate([jnp.ones((s, C_NOPE), F32), cos, cos, pad_hi], axis=1)
    sin_t = jnp.concatenate([jnp.zeros((s, C_NOPE), F32), sin, sin, pad_hi], axis=1)

    q1, k1, v1, gate1 = _proj1(h1, o_norm[0][None], cos_t, sin_t, w_lat, w_gate1,
                               o_q_norm[0][None], o_kv_norm[0][None],
                               flat(w_q), flat(w_qrot), flat(w_k), flat(w_v))
    o_c = _attn_c(jnp.swapaxes(r3(q1), 1, 2), r3(k1), jnp.swapaxes(r3(v1), 1, 2),
                  (C_NOPE + C_ROPE) ** -0.5)
    out = _out_proj([o_c.reshape(n, -1)], gate1, h1, o_w_o[0].astype(BF16), final_norm[None])
    return out.reshape(b, s, d)
```

```python
import functools
import math

import numpy as np
import jax
import jax.numpy as jnp
from jax import lax
from jax.experimental import pallas as pl
from jax.experimental.pallas import tpu as pltpu

F32 = jnp.float32
BF16 = jnp.bfloat16

CHUNK = 64
RMS_EPS = 1e-6
N_BUCKETS = 32
MAX_DISTANCE = 128
A_HEADS = 4
A_HEAD_DIM = 64
A_V_DIM = 2 * A_HEAD_DIM
B_HEADS = 8
B_HEAD_DIM = 64
IDX_HEADS = 8
IDX_DIM = 64
TOPK_MAX = 256
C_HEADS = 16
C_NOPE = 64
C_ROPE = 32
C_V = 64
Q_LORA = 384
KV_LORA = 256
ROPE_THETA = 10000.0

LANES = 128
BF16_ROWS = 16
ATT_TILE = 256
ROW_TILE = 512
VMEM_LIMIT = 56 * 1024 * 1024

C_GROUP = 8
LOG2E = math.log2(math.e)
NEG = -1e30
INT_MIN = -(2 ** 31)
NT_DIMS = (((1,), (1,)), ((), ()))
CHUNK_SHIFT = CHUNK.bit_length() - 1
assert 1 << CHUNK_SHIFT == CHUNK


def _t5_thresholds():
    nb = N_BUCKETS // 2
    max_exact = nb // 2
    n = np.arange(max_exact, MAX_DISTANCE + 1, dtype=np.float64)
    large = max_exact + (np.log(n / max_exact) / math.log(MAX_DISTANCE / max_exact)
                         * (nb - max_exact)).astype(np.int64)
    large = np.minimum(large, nb - 1)
    thr = [int(n[np.argmax(large >= b)]) for b in range(max_exact + 1, nb)]
    return max_exact, thr


T5_MAX_EXACT, T5_THRESHOLDS = _t5_thresholds()
T5_FAR_BUCKET = N_BUCKETS // 2 - 1


def _cparams(sem):
    return pltpu.CompilerParams(dimension_semantics=sem, vmem_limit_bytes=VMEM_LIMIT)


def _rms(x, g):
    return x * lax.rsqrt(jnp.mean(x * x, axis=-1, keepdims=True) + RMS_EPS) * g


def _silu(g):
    return g / (1.0 + jnp.exp(-g))


def _chunk_visible_t(shape):
    key = lax.broadcasted_iota(jnp.int32, shape, 0)
    qry = lax.broadcasted_iota(jnp.int32, shape, 1)
    return (key >> CHUNK_SHIFT) <= (qry >> CHUNK_SHIFT)


def _nt(w_t, x):
    return lax.dot_general(w_t, x, NT_DIMS, preferred_element_type=F32)


def _bias_tile_kernel(tab_ref, o_ref):
    h = pl.program_id(0)
    t = pl.program_id(1)
    shape = (ATT_TILE, ATT_TILE)
    key = lax.broadcasted_iota(jnp.int32, shape, 0)
    qry = lax.broadcasted_iota(jnp.int32, shape, 1)
    rel = key - qry - t * ATT_TILE
    n = jnp.abs(rel)
    large = jnp.full(shape, T5_MAX_EXACT, jnp.int32)
    for thr in T5_THRESHOLDS:
        large = large + jnp.where(n >= thr, 1, 0)
    bucket = jnp.where(rel > 0, N_BUCKETS // 2, 0) + jnp.where(n < T5_MAX_EXACT, n, large)
    val = jnp.full(shape, tab_ref[0, h], F32)
    for b in range(1, N_BUCKETS):
        val = jnp.where(bucket == b, tab_ref[b, h], val)
    visible = _chunk_visible_t(shape) | (t > 0)
    o_ref[0, 0] = jnp.where(visible, val * LOG2E, NEG)


def _bias_tiles(rel_bias):
    heads = rel_bias.shape[1]
    return pl.pallas_call(
        _bias_tile_kernel,
        grid=(heads, 2),
        in_specs=[pl.BlockSpec(memory_space=pltpu.SMEM)],
        out_specs=pl.BlockSpec((1, 1, ATT_TILE, ATT_TILE), lambda h, t: (h, t, 0, 0)),
        out_shape=jax.ShapeDtypeStruct((heads, 2, ATT_TILE, ATT_TILE), F32),
        compiler_params=_cparams(("arbitrary", "arbitrary")),
    )(rel_bias)


def _proj0_kernel(x_ref, g_ref, w_tok_ref, w_gate_ref, wqa_ref, wva_ref, wqb_ref, wqi_ref,
                  wvb_ref, wwi_ref, ka_ref, kb_ref, ki_ref, gate_ref,
                  qa_ref, va_ref, qb_ref, qi_ref, vb_ref, wi_ref, *, q_scale):
    hn = _rms(x_ref[0], g_ref[...]).astype(BF16)
    tok = jnp.dot(hn, w_tok_ref[...], preferred_element_type=F32)
    ka_ref[0] = tok[:, :512].astype(BF16)
    kb_ref[0] = tok[:, 512:512 + B_HEAD_DIM].astype(BF16)
    ki_ref[0] = tok[:, 512 + LANES:512 + LANES + IDX_DIM].astype(BF16)
    gate_ref[0] = jnp.dot(hn, w_gate_ref[...], preferred_element_type=F32).astype(BF16)
    qa_ref[0] = (_nt(wqa_ref[...], hn) * q_scale).astype(BF16)
    va_ref[0] = _nt(wva_ref[...], hn).astype(BF16)
    qb_ref[0] = (_nt(wqb_ref[...], hn) * q_scale).astype(BF16)
    qi_ref[0] = _nt(wqi_ref[...], hn).astype(BF16)
    vb_ref[0] = _nt(wvb_ref[...], hn).astype(BF16)
    wi_ref[0] = _nt(wwi_ref[...], hn)


def _proj0(x, g, w_tok, w_gate, wqa_t, wva_t, wqb_t, wqi_t, wvb_t, wwi_t, q_scale):
    b, s, d = x.shape
    tok = lambda c, dt: (pl.BlockSpec((1, ROW_TILE, c), lambda bi, i: (bi, i, 0)),
                         jax.ShapeDtypeStruct((b, s, c), dt))
    feat = lambda r, dt: (pl.BlockSpec((1, r, ROW_TILE), lambda bi, i: (bi, 0, i)),
                          jax.ShapeDtypeStruct((b, r, s), dt))
    full = lambda a: pl.BlockSpec(a.shape, lambda bi, i: (0, 0))
    outs = [tok(512, BF16), tok(B_HEAD_DIM, BF16), tok(IDX_DIM, BF16), tok(w_gate.shape[1], BF16),
            feat(wqa_t.shape[0], BF16), feat(wva_t.shape[0], BF16), feat(wqb_t.shape[0], BF16),
            feat(wqi_t.shape[0], BF16), feat(wvb_t.shape[0], BF16), feat(wwi_t.shape[0], F32)]
    weights = (w_tok, w_gate, wqa_t, wva_t, wqb_t, wqi_t, wvb_t, wwi_t)
    return pl.pallas_call(
        functools.partial(_proj0_kernel, q_scale=q_scale),
        grid=(b, s // ROW_TILE),
        in_specs=[pl.BlockSpec((1, ROW_TILE, d), lambda bi, i: (bi, i, 0)), full(g)]
        + [full(w) for w in weights],
        out_specs=[o[0] for o in outs],
        out_shape=[o[1] for o in outs],
        compiler_params=_cparams(("arbitrary", "arbitrary")),
    )(x, g, *weights)


def _proj1_kernel(x_ref, g_ref, cos_ref, sin_ref, cos_t_ref, sin_t_ref, w_lat_ref, w_gate_ref,
                  qn_ref, kvn_ref, wq_ref, wqrot_ref, wk_ref, wv_ref,
                  q_ref, k_ref, v_ref, gate_ref, *, q_scale):
    hn = _rms(x_ref[0], g_ref[...]).astype(BF16)
    lat = jnp.dot(hn, w_lat_ref[...], preferred_element_type=F32)
    q_lat = lat[:, :Q_LORA]
    kv_lat = lat[:, Q_LORA:Q_LORA + KV_LORA]
    kr = lat[:, Q_LORA + KV_LORA:Q_LORA + KV_LORA + LANES]
    kr_rot = lat[:, Q_LORA + KV_LORA + LANES:]
    k_rope = kr * cos_ref[...] + kr_rot * sin_ref[...]
    qn = _rms(q_lat, qn_ref[...]).astype(BF16)
    kvn = _rms(kv_lat, kvn_ref[...]).astype(BF16)
    cos_t = cos_t_ref[...] * q_scale
    sin_t = sin_t_ref[...] * q_scale
    for h in range(C_HEADS):
        sl = slice(h * LANES, (h + 1) * LANES)
        qa = _nt(wq_ref[sl, :], qn)
        qr = _nt(wqrot_ref[sl, :], qn)
        q_ref[0, sl, :] = (qa * cos_t + qr * sin_t).astype(BF16)
        kn = jnp.dot(kvn, wk_ref[:, sl], preferred_element_type=F32)
        k_ref[0, :, sl] = (kn + k_rope).astype(BF16)
    v_ref[0] = _nt(wv_ref[...], kvn).astype(BF16)
    gate_ref[0] = jnp.dot(hn, w_gate_ref[...], preferred_element_type=F32).astype(BF16)


def _proj1(x, g, cos, sin, cos_t, sin_t, w_lat, w_gate, qn_g, kvn_g, wq_t, wqrot_t, w_k, wv_t,
           q_scale):
    b, s, d = x.shape
    tok = lambda c, dt: (pl.BlockSpec((1, ROW_TILE, c), lambda bi, i: (bi, i, 0)),
                         jax.ShapeDtypeStruct((b, s, c), dt))
    feat = lambda r, dt: (pl.BlockSpec((1, r, ROW_TILE), lambda bi, i: (bi, 0, i)),
                          jax.ShapeDtypeStruct((b, r, s), dt))
    full = lambda a: pl.BlockSpec(a.shape, lambda bi, i: (0, 0))
    outs = [feat(C_HEADS * LANES, BF16), tok(C_HEADS * LANES, BF16), feat(C_HEADS * C_V, BF16),
            tok(w_gate.shape[1], BF16)]
    return pl.pallas_call(
        functools.partial(_proj1_kernel, q_scale=q_scale),
        grid=(b, s // ROW_TILE),
        in_specs=[pl.BlockSpec((1, ROW_TILE, d), lambda bi, i: (bi, i, 0)), full(g),
                  pl.BlockSpec((ROW_TILE, LANES), lambda bi, i: (i, 0)),
                  pl.BlockSpec((ROW_TILE, LANES), lambda bi, i: (i, 0)),
                  pl.BlockSpec((LANES, ROW_TILE), lambda bi, i: (0, i)),
                  pl.BlockSpec((LANES, ROW_TILE), lambda bi, i: (0, i)),
                  full(w_lat), full(w_gate), full(qn_g), full(kvn_g),
                  full(wq_t), full(wqrot_t), full(w_k), full(wv_t)],
        out_specs=[o[0] for o in outs],
        out_shape=[o[1] for o in outs],
        compiler_params=_cparams(("arbitrary", "arbitrary")),
    )(x, g, cos, sin, cos_t, sin_t, w_lat, w_gate, qn_g, kvn_g, wq_t, wqrot_t, w_k, wv_t)


def _out_kernel(*refs, n_mix, final_norm):
    mix_refs = refs[:n_mix]
    gate_ref, x_ref, w_ref = refs[n_mix:n_mix + 3]
    rest = refs[n_mix + 3:]
    sg = _silu(gate_ref[...].astype(F32))
    y = x_ref[...]
    off = 0
    for m_ref in mix_refs:
        c = m_ref.shape[1]
        a = (m_ref[...].astype(F32) * sg[:, off:off + c]).astype(BF16)
        y = y + jnp.dot(a, w_ref[off:off + c, :], preferred_element_type=F32)
        off += c
    if final_norm:
        g_ref, o_ref = rest
        o_ref[...] = _rms(y, g_ref[...])
    else:
        (o_ref,) = rest
        o_ref[...] = y


def _out_proj(mixes, gate, x2, w_o, final_g=None):
    n, d = x2.shape
    rows = lambda i: (i, 0)
    const = lambda i: (0, 0)
    in_specs = [pl.BlockSpec((ROW_TILE, m.shape[1]), rows) for m in mixes]
    in_specs += [pl.BlockSpec((ROW_TILE, gate.shape[1]), rows), pl.BlockSpec((ROW_TILE, d), rows),
                 pl.BlockSpec(w_o.shape, const)]
    args = list(mixes) + [gate, x2, w_o]
    if final_g is not None:
        in_specs.append(pl.BlockSpec((1, d), const))
        args.append(final_g)
    return pl.pallas_call(
        functools.partial(_out_kernel, n_mix=len(mixes), final_norm=final_g is not None),
        grid=(n // ROW_TILE,),
        in_specs=in_specs,
        out_specs=pl.BlockSpec((ROW_TILE, d), rows),
        out_shape=jax.ShapeDtypeStruct((n, d), F32),
        compiler_params=_cparams(("arbitrary",)),
    )(*args)


def _with_ones(v_t):
    return jnp.concatenate([v_t, jnp.ones((BF16_ROWS, v_t.shape[1]), v_t.dtype)], axis=0)


def _softmax_pv(s2, v_aug, m_ref, acc_ref, shift):
    m_old = m_ref[...]
    m_new = jnp.maximum(m_old, jnp.max(s2, axis=0, keepdims=True) + shift)
    p = jnp.exp2(s2 - (m_new - shift))
    alpha = jnp.exp2(m_old - m_new)
    acc_ref[...] = alpha * acc_ref[...] + jnp.dot(v_aug, p.astype(v_aug.dtype),
                                                  preferred_element_type=F32)
    m_ref[...] = m_new


def _init_flash(m_ref, acc_ref):
    m_ref[...] = jnp.full(m_ref.shape, -jnp.inf, F32)
    acc_ref[...] = jnp.zeros(acc_ref.shape, F32)


def _normalised(acc, dv):
    return acc[:dv] / acc[dv:dv + 1]


def _key_tile_loop(qt, far_fn, near_fn, diag_fn):
    def far_body(j, carry):
        far_fn(j)
        return carry
    lax.fori_loop(0, jnp.maximum(qt - 1, 0), far_body, 0)

    @pl.when(qt >= 1)
    def _():
        near_fn(qt - 1)

    diag_fn(qt)


def _attn_a_kernel(cfar_ref, q_ref, k_ref, v_ref, bias_ref, lam_ref, g_ref, o_ref,
                   m_ref, acc_ref, *, lam_init):
    qt = pl.program_id(1)
    t = ATT_TILE
    _init_flash(m_ref, acc_ref)

    def tile(j, bias_idx):
        k0 = pl.multiple_of(j * t, t)
        scores = []
        for h in range(A_HEADS):
            kt = k_ref[0, pl.ds(k0, t), h * LANES:(h + 1) * LANES]
            for mp in range(2):
                c = 2 * h + mp
                scores.append(jnp.dot(kt, q_ref[0, c * LANES:(c + 1) * LANES, :],
                                      preferred_element_type=F32))
        for h in range(A_HEADS):
            v_aug = _with_ones(v_ref[0, h * A_V_DIM:(h + 1) * A_V_DIM, pl.ds(k0, t)])
            for mp in range(2):
                c = 2 * h + mp
                if bias_idx is None:
                    s2, shift = scores[c], cfar_ref[h]
                else:
                    s2, shift = bias_ref[h, bias_idx] + scores[c], 0.0
                _softmax_pv(s2, v_aug, m_ref.at[c], acc_ref.at[c], shift)

    _key_tile_loop(qt, lambda j: tile(j, None), lambda j: tile(j, 1), lambda j: tile(j, 0))

    lam_v = lam_ref[...]
    lam = (jnp.exp(jnp.sum(lam_v[0:1] * lam_v[1:2], axis=-1, keepdims=True))
           - jnp.exp(jnp.sum(lam_v[2:3] * lam_v[3:4], axis=-1, keepdims=True)) + lam_init)
    outs = []
    for h in range(A_HEADS):
        o = (_normalised(acc_ref[2 * h], A_V_DIM)
             - lam * _normalised(acc_ref[2 * h + 1], A_V_DIM))
        inv = lax.rsqrt(jnp.mean(o * o, axis=0, keepdims=True) + RMS_EPS)
        outs.append(o * inv * (g_ref[...] * (1.0 - lam_init)))
    o_ref[0] = jnp.concatenate(outs, axis=0).T.astype(o_ref.dtype)


def _attn_a(cfar, qa_t, ka, va_t, bias, lam_vecs, subln_col, lam_init):
    b, s, _ = ka.shape
    nq = s // ATT_TILE
    chains = 2 * A_HEADS
    return pl.pallas_call(
        functools.partial(_attn_a_kernel, lam_init=lam_init),
        grid=(b, nq),
        in_specs=[pl.BlockSpec(memory_space=pltpu.SMEM),
                  pl.BlockSpec((1, chains * LANES, ATT_TILE), lambda bi, qi: (bi, 0, qi)),
                  pl.BlockSpec((1, s, ka.shape[2]), lambda bi, qi: (bi, 0, 0)),
                  pl.BlockSpec((1, va_t.shape[1], s), lambda bi, qi: (bi, 0, 0)),
                  pl.BlockSpec(bias.shape, lambda bi, qi: (0, 0, 0, 0)),
                  pl.BlockSpec(lam_vecs.shape, lambda bi, qi: (0, 0)),
                  pl.BlockSpec(subln_col.shape, lambda bi, qi: (0, 0))],
        out_specs=pl.BlockSpec((1, ATT_TILE, A_HEADS * A_V_DIM), lambda bi, qi: (bi, qi, 0)),
        out_shape=jax.ShapeDtypeStruct((b, s, A_HEADS * A_V_DIM), BF16),
        scratch_shapes=[pltpu.VMEM((chains, 1, ATT_TILE), F32),
                        pltpu.VMEM((chains, A_V_DIM + BF16_ROWS, ATT_TILE), F32)],
        compiler_params=_cparams(("arbitrary", "arbitrary")),
    )(cfar, qa_t, ka, va_t, bias, lam_vecs, subln_col)


def _attn_b_kernel(cfar_ref, qi_ref, ki_ref, wi_ref, qb_ref, kb_ref, vb_ref, bias_ref, o_ref,
                   keys_ref, cut_ref, m_ref, acc_ref, *, seq_len, top_k):
    qt = pl.program_id(1)
    n_tiles = qt + 1
    t = ATT_TILE
    groups = t // 8
    chunk_ok = _chunk_visible_t((t, t))
    key_in_tile = lax.broadcasted_iota(jnp.int32, (t, t), 0)

    def tile_start(j):
        return pl.multiple_of(j * t, t)

    w = wi_ref[0]

    def score_tile(j, carry):
        k0 = tile_start(j)
        kt = ki_ref[0, pl.ds(k0, t), :]
        logits = [jnp.dot(kt, qi_ref[0, hh * IDX_DIM:(hh + 1) * IDX_DIM, :],
                          preferred_element_type=F32) for hh in range(IDX_HEADS)]
        score = None
        for hh in range(IDX_HEADS):
            term = w[hh:hh + 1, :] * jnp.maximum(logits[hh], 0.0)
            score = term if score is None else score + term
        score = jnp.where(chunk_ok | (j < qt), score, -jnp.inf)
        bits = lax.bitcast_convert_type(score, jnp.int32)
        keys_ref[pl.ds(k0, t), :] = jnp.where(bits < 0, (bits ^ 0x7FFFFFFF) + 1, bits)
        return carry

    lax.fori_loop(0, n_tiles, score_tile, 0)

    def count(pred_fn):
        def body(j, acc):
            k0 = tile_start(j)
            hit = pred_fn(keys_ref[pl.ds(k0, t), :], k0 + key_in_tile)
            return acc + jnp.sum(hit.reshape(groups, 8, t), axis=0)
        acc = lax.fori_loop(0, n_tiles, body, jnp.zeros((8, t), F32))
        return jnp.sum(acc, axis=0, keepdims=True)

    kf = float(top_k)

    def value_step(i, thr):
        cand = thr + lax.shift_left(jnp.int32(1), jnp.asarray(31 - i, jnp.int32))
        cnt = count(lambda blk, idx: jnp.where(blk >= cand, 1.0, 0.0))
        return jnp.where(cnt >= kf, cand, thr)

    thr = lax.fori_loop(0, 32, value_step, jnp.full((1, t), INT_MIN, jnp.int32))
    n_gt = count(lambda blk, idx: jnp.where(blk > thr, 1.0, 0.0))
    n_eq = count(lambda blk, idx: jnp.where(blk == thr, 1.0, 0.0))
    need = kf - n_gt
    cut_ref[...] = jnp.full(cut_ref.shape, seq_len, jnp.int32)
    any_tie = jnp.max(jnp.where(n_eq != need, 1.0, 0.0)) > 0.0

    @pl.when(any_tie)
    def _():
        n_bits = max(1, (seq_len - 1).bit_length())

        def index_step(i, c):
            cand = c + lax.shift_left(jnp.int32(1), jnp.asarray(n_bits - 1 - i, jnp.int32))
            below = count(lambda blk, idx: jnp.where(
                blk == thr, jnp.where(idx < cand, 1.0, 0.0), 0.0))
            return jnp.where(below < need, cand, c)

        c_star = lax.fori_loop(0, n_bits, index_step, jnp.zeros((1, t), jnp.int32))
        cut_ref[...] = jnp.broadcast_to(c_star, cut_ref.shape)

    cut = cut_ref[0:1, :]

    def mask_tile(j, carry):
        k0 = tile_start(j)
        blk = keys_ref[pl.ds(k0, t), :]
        tie_keep = jnp.where(k0 + key_in_tile <= cut, 0.0, NEG)
        mb = jnp.where(blk > thr, 0.0, jnp.where(blk == thr, tie_keep, NEG))
        keys_ref[pl.ds(k0, t), :] = lax.bitcast_convert_type(mb, jnp.int32)
        return carry

    lax.fori_loop(0, n_tiles, mask_tile, 0)

    _init_flash(m_ref, acc_ref)

    def attend(j, bias_idx):
        k0 = tile_start(j)
        kt = kb_ref[0, pl.ds(k0, t), :]
        scores = [jnp.dot(kt, qb_ref[0, hh * B_HEAD_DIM:(hh + 1) * B_HEAD_DIM, :],
                          preferred_element_type=F32) for hh in range(B_HEADS)]
        v_aug = _with_ones(vb_ref[0, :, pl.ds(k0, t)])
        mb = lax.bitcast_convert_type(keys_ref[pl.ds(k0, t), :], F32)
        for hh in range(B_HEADS):
            if bias_idx is None:
                s2, shift = mb + scores[hh], cfar_ref[hh]
            else:
                s2, shift = (mb + bias_ref[hh, bias_idx]) + scores[hh], 0.0
            _softmax_pv(s2, v_aug, m_ref.at[hh], acc_ref.at[hh], shift)

    _key_tile_loop(qt, lambda j: attend(j, None), lambda j: attend(j, 1), lambda j: attend(j, 0))

    o_t = jnp.concatenate([_normalised(acc_ref[hh], B_HEAD_DIM) for hh in range(B_HEADS)], axis=0)
    o_ref[0] = o_t.T.astype(o_ref.dtype)


def _attn_b(cfar, qi_t, ki, wi_t, qb_t, kb, vb_t, bias, top_k):
    b, s, _ = kb.shape
    nq = s // ATT_TILE
    qmap = lambda bi, qt: (bi, 0, qt)
    kmap = lambda bi, qt: (bi, 0, 0)
    return pl.pallas_call(
        functools.partial(_attn_b_kernel, seq_len=s, top_k=top_k),
        grid=(b, nq),
        in_specs=[pl.BlockSpec(memory_space=pltpu.SMEM),
                  pl.BlockSpec((1, qi_t.shape[1], ATT_TILE), qmap),
                  pl.BlockSpec((1, s, ki.shape[2]), kmap),
                  pl.BlockSpec((1, wi_t.shape[1], ATT_TILE), qmap),
                  pl.BlockSpec((1, qb_t.shape[1], ATT_TILE), qmap),
                  pl.BlockSpec((1, s, kb.shape[2]), kmap),
                  pl.BlockSpec((1, vb_t.shape[1], s), kmap),
                  pl.BlockSpec(bias.shape, lambda bi, qt: (0, 0, 0, 0))],
        out_specs=pl.BlockSpec((1, ATT_TILE, B_HEADS * B_HEAD_DIM), lambda bi, qt: (bi, qt, 0)),
        out_shape=jax.ShapeDtypeStruct((b, s, B_HEADS * B_HEAD_DIM), BF16),
        scratch_shapes=[pltpu.VMEM((s, ATT_TILE), jnp.int32),
                        pltpu.VMEM((8, ATT_TILE), jnp.int32),
                        pltpu.VMEM((B_HEADS, 1, ATT_TILE), F32),
                        pltpu.VMEM((B_HEADS, B_HEAD_DIM + BF16_ROWS, ATT_TILE), F32)],
        compiler_params=_cparams(("arbitrary", "arbitrary")),
    )(cfar, qi_t, ki, wi_t, qb_t, kb, vb_t, bias)


def _attn_c_kernel(q_ref, k_ref, v_ref, o_ref, m_ref, acc_ref):
    qt = pl.program_id(2)
    t = ATT_TILE
    chunk_ok = _chunk_visible_t((t, t))
    _init_flash(m_ref, acc_ref)

    def tile(j, diagonal):
        k0 = pl.multiple_of(j * t, t)
        scores = []
        for g in range(C_GROUP):
            kt = k_ref[0, pl.ds(k0, t), g * LANES:(g + 1) * LANES]
            scores.append(jnp.dot(kt, q_ref[0, g * LANES:(g + 1) * LANES, :],
                                  preferred_element_type=F32))
        for g in range(C_GROUP):
            s2 = jnp.where(chunk_ok, scores[g], NEG) if diagonal else scores[g]
            v_aug = _with_ones(v_ref[0, g * C_V:(g + 1) * C_V, pl.ds(k0, t)])
            _softmax_pv(s2, v_aug, m_ref.at[g], acc_ref.at[g], 0.0)

    def far_body(j, carry):
        tile(j, False)
        return carry

    lax.fori_loop(0, qt, far_body, 0)
    tile(qt, True)
    o_t = jnp.concatenate([_normalised(acc_ref[g], C_V) for g in range(C_GROUP)], axis=0)
    o_ref[0] = o_t.T.astype(o_ref.dtype)


def _attn_c(q_t, k, v_t):
    b, s, _ = k.shape
    nq = s // ATT_TILE
    groups = C_HEADS // C_GROUP
    return pl.pallas_call(
        _attn_c_kernel,
        grid=(b, groups, nq),
        in_specs=[pl.BlockSpec((1, C_GROUP * LANES, ATT_TILE), lambda bi, p, qi: (bi, p, qi)),
                  pl.BlockSpec((1, s, C_GROUP * LANES), lambda bi, p, qi: (bi, 0, p)),
                  pl.BlockSpec((1, C_GROUP * C_V, s), lambda bi, p, qi: (bi, p, 0))],
        out_specs=pl.BlockSpec((1, ATT_TILE, C_GROUP * C_V), lambda bi, p, qi: (bi, qi, p)),
        out_shape=jax.ShapeDtypeStruct((b, s, C_HEADS * C_V), BF16),
        scratch_shapes=[pltpu.VMEM((C_GROUP, 1, ATT_TILE), F32),
                        pltpu.VMEM((C_GROUP, C_V + BF16_ROWS, ATT_TILE), F32)],
        compiler_params=_cparams(("arbitrary", "arbitrary", "arbitrary")),
    )(q_t, k, v_t)


def _rot_cols(w):
    half = w.shape[-1] // 2
    return jnp.concatenate([-w[..., half:], w[..., :half]], axis=-1)


def _rope_slot(w_rope):
    return jnp.pad(w_rope, ((0, 0), (0, 0), (C_NOPE, LANES - C_NOPE - C_ROPE)))


def _t_bf16(w):
    return w.T.astype(BF16)


def kernel(x, rel_bias, e_norm, e_w_in, e_lam_q1, e_lam_k1, e_lam_q2, e_lam_k2, e_subln, e_w_o,
           o_norm, o_w_in, o_q_norm, o_w_uq, o_kv_norm, o_w_ukv, o_w_o, final_norm):
    b, s, d = x.shape
    assert s % ROW_TILE == 0 and s % ATT_TILE == 0 and ATT_TILE % CHUNK == 0
    assert ATT_TILE >= MAX_DISTANCE, "key tiles two or more before the diagonal must be all-far"
    top_k = min(TOPK_MAX, s // 4)
    assert top_k <= ATT_TILE, "every query must see at least top_k keys inside its own window"
    n = b * s

    bias = _bias_tiles(rel_bias)
    cfar = rel_bias[T5_FAR_BUCKET] * LOG2E

    w_in = e_w_in[0]
    sizes = (512, 512, 512, 512, 64, 64, 512, 64, 8, 1024)
    offs = np.concatenate([[0], np.cumsum(sizes)])
    col = lambda i: w_in[:, offs[i]:offs[i + 1]]
    zeros64 = jnp.zeros((d, LANES - B_HEAD_DIM), F32)
    w_tok = jnp.concatenate([col(1), col(4), zeros64, col(7), zeros64], axis=1).astype(BF16)
    wqa = col(0).reshape(d, A_HEADS, 2, A_HEAD_DIM)
    zq = jnp.zeros((d, A_HEADS, A_HEAD_DIM), F32)
    wqa = jnp.stack([jnp.concatenate([wqa[:, :, 0], zq], axis=-1),
                     jnp.concatenate([zq, wqa[:, :, 1]], axis=-1)], axis=2)
    wqa_t = _t_bf16(wqa.reshape(d, 2 * A_HEADS * LANES))
    wwi_t = _t_bf16(jnp.pad(col(8), ((0, 0), (0, BF16_ROWS - IDX_HEADS))))
    ka, kb, ki, gate0, qa_t, va_t, qb_t, qi_t, vb_t, wi_t = _proj0(
        x, e_norm[0][None], w_tok, col(9).astype(BF16), wqa_t, _t_bf16(col(2)), _t_bf16(col(3)),
        _t_bf16(col(6)), _t_bf16(col(5)), wwi_t, (A_HEAD_DIM ** -0.5) * LOG2E)

    lam_init = 0.8 - 0.6 * math.exp(-0.3 * 0)
    lam_vecs = jnp.stack([e_lam_q1[0], e_lam_k1[0], e_lam_q2[0], e_lam_k2[0]])
    o_a = _attn_a(cfar[:A_HEADS], qa_t, ka, va_t, bias[:A_HEADS], lam_vecs,
                  e_subln[0][:, None], lam_init)
    o_b = _attn_b(cfar[A_HEADS:], qi_t, ki, wi_t, qb_t, kb, vb_t, bias[A_HEADS:], top_k)
    x2 = x.reshape(n, d)
    h1 = _out_proj([o_a.reshape(n, -1), o_b.reshape(n, -1)], gate0.reshape(n, -1), x2,
                   e_w_o[0].astype(BF16))

    w_in1 = o_w_in[0]
    w_ql = w_in1[:, :Q_LORA]
    w_kvl = w_in1[:, Q_LORA:Q_LORA + KV_LORA]
    w_kr = w_in1[:, Q_LORA + KV_LORA:Q_LORA + KV_LORA + C_ROPE]
    w_gate1 = w_in1[:, Q_LORA + KV_LORA + C_ROPE:].astype(BF16)
    kr_slot = _rope_slot(w_kr[:, None, :])[:, 0]
    kr_rot_slot = _rope_slot(_rot_cols(w_kr)[:, None, :])[:, 0]
    w_lat = jnp.concatenate([w_ql, w_kvl, kr_slot, kr_rot_slot], axis=1).astype(BF16)

    w_uq = o_w_uq[0].reshape(Q_LORA, C_HEADS, C_NOPE + C_ROPE)
    w_q = jnp.pad(w_uq[..., :C_NOPE], ((0, 0), (0, 0), (0, LANES - C_NOPE))) \
        + _rope_slot(w_uq[..., C_NOPE:])
    w_qrot = _rope_slot(_rot_cols(w_uq[..., C_NOPE:]))
    w_ukv = o_w_ukv[0].reshape(KV_LORA, C_HEADS, C_NOPE + C_V)
    w_k = jnp.pad(w_ukv[..., :C_NOPE], ((0, 0), (0, 0), (0, LANES - C_NOPE)))
    w_v = w_ukv[..., C_NOPE:]
    flat = lambda w: w.reshape(w.shape[0], -1)

    inv = ROPE_THETA ** (-jnp.arange(0, C_ROPE, 2, dtype=F32) / C_ROPE)
    ang = jnp.arange(s, dtype=F32)[:, None] * inv[None, :]
    cos, sin = jnp.cos(ang), jnp.sin(ang)
    pad_hi = jnp.zeros((s, LANES - C_NOPE - C_ROPE), F32)
    cos_tab = jnp.concatenate([jnp.ones((s, C_NOPE), F32), cos, cos, pad_hi], axis=1)
    sin_tab = jnp.concatenate([jnp.zeros((s, C_NOPE), F32), sin, sin, pad_hi], axis=1)

    q1_t, k1, v1_t, gate1 = _proj1(
        h1.reshape(b, s, d), o_norm[0][None], cos_tab, sin_tab, cos_tab.T, sin_tab.T, w_lat,
        w_gate1, o_q_norm[0][None], o_kv_norm[0][None], _t_bf16(flat(w_q)), _t_bf16(flat(w_qrot)),
        flat(w_k).astype(BF16), _t_bf16(flat(w_v)), ((C_NOPE + C_ROPE) ** -0.5) * LOG2E)
    o_c = _attn_c(q1_t, k1, v1_t)
    out = _out_proj([o_c.reshape(n, -1)], gate1.reshape(n, -1), h1, o_w_o[0].astype(BF16),
                    final_norm[None])
    return out.reshape(b, s, d)
```

```python
import functools
import math

import numpy as np
import jax
import jax.numpy as jnp
from jax import lax
from jax.experimental import pallas as pl
from jax.experimental.pallas import tpu as pltpu

F32 = jnp.float32
BF16 = jnp.bfloat16

CHUNK = 64
RMS_EPS = 1e-6
N_BUCKETS = 32
MAX_DISTANCE = 128
A_HEADS = 4
A_HEAD_DIM = 64
A_V_DIM = 2 * A_HEAD_DIM
B_HEADS = 8
B_HEAD_DIM = 64
IDX_HEADS = 8
IDX_DIM = 64
TOPK_MAX = 256
C_HEADS = 16
C_NOPE = 64
C_ROPE = 32
C_V = 64
Q_LORA = 384
KV_LORA = 256
ROPE_THETA = 10000.0

LANES = 128
BF16_ROWS = 16
ATT_TILE = 256
ROW_TILE = 512
VMEM_LIMIT = 56 * 1024 * 1024

C_GROUP = 8
COUNT_ROWS = 32
LOG2E = math.log2(math.e)
NEG = -1e30
INT_MIN = -(2 ** 31)
NT_DIMS = (((1,), (1,)), ((), ()))
CHUNK_SHIFT = CHUNK.bit_length() - 1
assert 1 << CHUNK_SHIFT == CHUNK


def _t5_thresholds():
    nb = N_BUCKETS // 2
    max_exact = nb // 2
    n = np.arange(max_exact, MAX_DISTANCE + 1, dtype=np.float64)
    large = max_exact + (np.log(n / max_exact) / math.log(MAX_DISTANCE / max_exact)
                         * (nb - max_exact)).astype(np.int64)
    large = np.minimum(large, nb - 1)
    thr = [int(n[np.argmax(large >= b)]) for b in range(max_exact + 1, nb)]
    return max_exact, thr


T5_MAX_EXACT, T5_THRESHOLDS = _t5_thresholds()
T5_FAR_BUCKET = N_BUCKETS // 2 - 1


def _cparams(sem):
    return pltpu.CompilerParams(dimension_semantics=sem, vmem_limit_bytes=VMEM_LIMIT)


def _rms(x, g):
    return x * lax.rsqrt(jnp.mean(x * x, axis=-1, keepdims=True) + RMS_EPS) * g


def _silu(g):
    return g / (1.0 + jnp.exp(-g))


def _chunk_visible_t(shape):
    key = lax.broadcasted_iota(jnp.int32, shape, 0)
    qry = lax.broadcasted_iota(jnp.int32, shape, 1)
    return (key >> CHUNK_SHIFT) <= (qry >> CHUNK_SHIFT)


def _nt(w_t, x):
    return lax.dot_general(w_t, x, NT_DIMS, preferred_element_type=F32)


def _bias_tile_kernel(tab_ref, o_ref):
    h = pl.program_id(0)
    t = pl.program_id(1)
    shape = (ATT_TILE, ATT_TILE)
    key = lax.broadcasted_iota(jnp.int32, shape, 0)
    qry = lax.broadcasted_iota(jnp.int32, shape, 1)
    rel = key - qry - t * ATT_TILE
    n = jnp.abs(rel)
    large = jnp.full(shape, T5_MAX_EXACT, jnp.int32)
    for thr in T5_THRESHOLDS:
        large = large + jnp.where(n >= thr, 1, 0)
    bucket = jnp.where(rel > 0, N_BUCKETS // 2, 0) + jnp.where(n < T5_MAX_EXACT, n, large)
    val = jnp.full(shape, tab_ref[0, h], F32)
    for b in range(1, N_BUCKETS):
        val = jnp.where(bucket == b, tab_ref[b, h], val)
    visible = _chunk_visible_t(shape) | (t > 0)
    o_ref[0, 0] = jnp.where(visible, val * LOG2E, NEG)


def _bias_tiles(rel_bias):
    heads = rel_bias.shape[1]
    return pl.pallas_call(
        _bias_tile_kernel,
        grid=(heads, 2),
        in_specs=[pl.BlockSpec(memory_space=pltpu.SMEM)],
        out_specs=pl.BlockSpec((1, 1, ATT_TILE, ATT_TILE), lambda h, t: (h, t, 0, 0)),
        out_shape=jax.ShapeDtypeStruct((heads, 2, ATT_TILE, ATT_TILE), F32),
        compiler_params=_cparams(("arbitrary", "arbitrary")),
    )(rel_bias)


def _proj0_kernel(x_ref, g_ref, w_tok_ref, w_gate_ref, wqa_ref, wva_ref, wqb_ref, wqi_ref,
                  wvb_ref, wwi_ref, ka_ref, kb_ref, ki_ref, gate_ref,
                  qa_ref, va_ref, qb_ref, qi_ref, vb_ref, wi_ref, *, q_scale):
    hn = _rms(x_ref[0], g_ref[...]).astype(BF16)
    tok = jnp.dot(hn, w_tok_ref[...], preferred_element_type=F32)
    ka_ref[0] = tok[:, :512].astype(BF16)
    kb_ref[0] = tok[:, 512:512 + B_HEAD_DIM].astype(BF16)
    ki_ref[0] = tok[:, 512 + LANES:512 + LANES + IDX_DIM].astype(BF16)
    gate_ref[0] = jnp.dot(hn, w_gate_ref[...], preferred_element_type=F32).astype(BF16)
    qa_ref[0] = (_nt(wqa_ref[...], hn) * q_scale).astype(BF16)
    va_ref[0] = _nt(wva_ref[...], hn).astype(BF16)
    qb_ref[0] = (_nt(wqb_ref[...], hn) * q_scale).astype(BF16)
    qi_ref[0] = _nt(wqi_ref[...], hn).astype(BF16)
    vb_ref[0] = _nt(wvb_ref[...], hn).astype(BF16)
    wi_ref[0] = _nt(wwi_ref[...], hn)


def _proj0(x, g, w_tok, w_gate, wqa_t, wva_t, wqb_t, wqi_t, wvb_t, wwi_t, q_scale):
    b, s, d = x.shape
    tok = lambda c, dt: (pl.BlockSpec((1, ROW_TILE, c), lambda bi, i: (bi, i, 0)),
                         jax.ShapeDtypeStruct((b, s, c), dt))
    feat = lambda r, dt: (pl.BlockSpec((1, r, ROW_TILE), lambda bi, i: (bi, 0, i)),
                          jax.ShapeDtypeStruct((b, r, s), dt))
    full = lambda a: pl.BlockSpec(a.shape, lambda bi, i: (0, 0))
    outs = [tok(512, BF16), tok(B_HEAD_DIM, BF16), tok(IDX_DIM, BF16), tok(w_gate.shape[1], BF16),
            feat(wqa_t.shape[0], BF16), feat(wva_t.shape[0], BF16), feat(wqb_t.shape[0], BF16),
            feat(wqi_t.shape[0], BF16), feat(wvb_t.shape[0], BF16), feat(wwi_t.shape[0], F32)]
    weights = (w_tok, w_gate, wqa_t, wva_t, wqb_t, wqi_t, wvb_t, wwi_t)
    return pl.pallas_call(
        functools.partial(_proj0_kernel, q_scale=q_scale),
        grid=(b, s // ROW_TILE),
        in_specs=[pl.BlockSpec((1, ROW_TILE, d), lambda bi, i: (bi, i, 0)), full(g)]
        + [full(w) for w in weights],
        out_specs=[o[0] for o in outs],
        out_shape=[o[1] for o in outs],
        compiler_params=_cparams(("arbitrary", "arbitrary")),
    )(x, g, *weights)


def _proj1_kernel(x_ref, g_ref, cos_ref, sin_ref, cos_t_ref, sin_t_ref, w_lat_ref, w_gate_ref,
                  qn_ref, kvn_ref, wq_ref, wqrot_ref, wk_ref, wv_ref,
                  q_ref, k_ref, v_ref, gate_ref, *, q_scale):
    hn = _rms(x_ref[0], g_ref[...]).astype(BF16)
    lat = jnp.dot(hn, w_lat_ref[...], preferred_element_type=F32)
    q_lat = lat[:, :Q_LORA]
    kv_lat = lat[:, Q_LORA:Q_LORA + KV_LORA]
    kr = lat[:, Q_LORA + KV_LORA:Q_LORA + KV_LORA + LANES]
    kr_rot = lat[:, Q_LORA + KV_LORA + LANES:]
    k_rope = kr * cos_ref[...] + kr_rot * sin_ref[...]
    qn = _rms(q_lat, qn_ref[...]).astype(BF16)
    kvn = _rms(kv_lat, kvn_ref[...]).astype(BF16)
    cos_t = cos_t_ref[...] * q_scale
    sin_t = sin_t_ref[...] * q_scale
    qa = _nt(wq_ref[...], qn)
    qr = _nt(wqrot_ref[...], qn)
    kn = jnp.dot(kvn, wk_ref[...], preferred_element_type=F32)
    for h in range(C_HEADS):
        sl = slice(h * LANES, (h + 1) * LANES)
        q_ref[0, sl, :] = (qa[sl] * cos_t + qr[sl] * sin_t).astype(BF16)
        k_ref[0, :, sl] = (kn[:, sl] + k_rope).astype(BF16)
    v_ref[0] = _nt(wv_ref[...], kvn).astype(BF16)
    gate_ref[0] = jnp.dot(hn, w_gate_ref[...], preferred_element_type=F32).astype(BF16)


def _proj1(x, g, cos, sin, cos_t, sin_t, w_lat, w_gate, qn_g, kvn_g, wq_t, wqrot_t, w_k, wv_t,
           q_scale):
    b, s, d = x.shape
    tok = lambda c, dt: (pl.BlockSpec((1, ROW_TILE, c), lambda bi, i: (bi, i, 0)),
                         jax.ShapeDtypeStruct((b, s, c), dt))
    feat = lambda r, dt: (pl.BlockSpec((1, r, ROW_TILE), lambda bi, i: (bi, 0, i)),
                          jax.ShapeDtypeStruct((b, r, s), dt))
    full = lambda a: pl.BlockSpec(a.shape, lambda bi, i: (0, 0))
    outs = [feat(C_HEADS * LANES, BF16), tok(C_HEADS * LANES, BF16), feat(C_HEADS * C_V, BF16),
            tok(w_gate.shape[1], BF16)]
    return pl.pallas_call(
        functools.partial(_proj1_kernel, q_scale=q_scale),
        grid=(b, s // ROW_TILE),
        in_specs=[pl.BlockSpec((1, ROW_TILE, d), lambda bi, i: (bi, i, 0)), full(g),
                  pl.BlockSpec((ROW_TILE, LANES), lambda bi, i: (i, 0)),
                  pl.BlockSpec((ROW_TILE, LANES), lambda bi, i: (i, 0)),
                  pl.BlockSpec((LANES, ROW_TILE), lambda bi, i: (0, i)),
                  pl.BlockSpec((LANES, ROW_TILE), lambda bi, i: (0, i)),
                  full(w_lat), full(w_gate), full(qn_g), full(kvn_g),
                  full(wq_t), full(wqrot_t), full(w_k), full(wv_t)],
        out_specs=[o[0] for o in outs],
        out_shape=[o[1] for o in outs],
        compiler_params=_cparams(("arbitrary", "arbitrary")),
    )(x, g, cos, sin, cos_t, sin_t, w_lat, w_gate, qn_g, kvn_g, wq_t, wqrot_t, w_k, wv_t)


def _out_kernel(*refs, n_mix, final_norm):
    mix_refs = refs[:n_mix]
    gate_ref, x_ref, w_ref = refs[n_mix:n_mix + 3]
    rest = refs[n_mix + 3:]
    sg = _silu(gate_ref[...].astype(F32))
    y = x_ref[...]
    off = 0
    for m_ref in mix_refs:
        c = m_ref.shape[1]
        a = (m_ref[...].astype(F32) * sg[:, off:off + c]).astype(BF16)
        y = y + jnp.dot(a, w_ref[off:off + c, :], preferred_element_type=F32)
        off += c
    if final_norm:
        g_ref, o_ref = rest
        o_ref[...] = _rms(y, g_ref[...])
    else:
        (o_ref,) = rest
        o_ref[...] = y


def _out_proj(mixes, gate, x2, w_o, final_g=None):
    n, d = x2.shape
    rows = lambda i: (i, 0)
    const = lambda i: (0, 0)
    in_specs = [pl.BlockSpec((ROW_TILE, m.shape[1]), rows) for m in mixes]
    in_specs += [pl.BlockSpec((ROW_TILE, gate.shape[1]), rows), pl.BlockSpec((ROW_TILE, d), rows),
                 pl.BlockSpec(w_o.shape, const)]
    args = list(mixes) + [gate, x2, w_o]
    if final_g is not None:
        in_specs.append(pl.BlockSpec((1, d), const))
        args.append(final_g)
    return pl.pallas_call(
        functools.partial(_out_kernel, n_mix=len(mixes), final_norm=final_g is not None),
        grid=(n // ROW_TILE,),
        in_specs=in_specs,
        out_specs=pl.BlockSpec((ROW_TILE, d), rows),
        out_shape=jax.ShapeDtypeStruct((n, d), F32),
        compiler_params=_cparams(("arbitrary",)),
    )(*args)


def _with_ones(v_t):
    return jnp.concatenate([v_t, jnp.ones((BF16_ROWS, v_t.shape[1]), v_t.dtype)], axis=0)


def _softmax_pv(s2, v_aug, m_ref, acc_ref, shift):
    m_old = m_ref[...]
    m_new = jnp.maximum(m_old, jnp.max(s2, axis=0, keepdims=True) + shift)
    p = jnp.exp2(s2 - (m_new - shift))
    alpha = jnp.exp2(m_old - m_new)
    acc_ref[...] = alpha * acc_ref[...] + jnp.dot(v_aug, p.astype(v_aug.dtype),
                                                  preferred_element_type=F32)
    m_ref[...] = m_new


def _init_flash(m_ref, acc_ref):
    m_ref[...] = jnp.full(m_ref.shape, -jnp.inf, F32)
    acc_ref[...] = jnp.zeros(acc_ref.shape, F32)


def _normalised(acc, dv):
    return acc[:dv] / acc[dv:dv + 1]


def _far_tiles(n_far, qk_fn, consume_fn):
    def pair_body(i, carry):
        ja, jb = 2 * i, 2 * i + 1
        sa = qk_fn(ja)
        sb = qk_fn(jb)
        consume_fn(ja, sa)
        consume_fn(jb, sb)
        return carry
    lax.fori_loop(0, n_far >> 1, pair_body, 0)

    @pl.when((n_far & 1) == 1)
    def _():
        consume_fn(n_far - 1, qk_fn(n_far - 1))


def _key_tile_loop(qt, qk_fn, consume_fn):
    _far_tiles(jnp.maximum(qt - 1, 0), qk_fn, lambda j, sc: consume_fn(j, sc, None))

    @pl.when(qt >= 1)
    def _():
        consume_fn(qt - 1, qk_fn(qt - 1), 1)

    consume_fn(qt, qk_fn(qt), 0)


def _attn_a_kernel(cfar_ref, q_ref, k_ref, v_ref, bias_ref, lam_ref, g_ref, o_ref,
                   m_ref, acc_ref, *, lam_init):
    qt = pl.program_id(1)
    t = ATT_TILE
    _init_flash(m_ref, acc_ref)

    def qk(j):
        k0 = pl.multiple_of(j * t, t)
        scores = []
        for h in range(A_HEADS):
            kt = k_ref[0, pl.ds(k0, t), h * LANES:(h + 1) * LANES]
            for mp in range(2):
                c = 2 * h + mp
                scores.append(jnp.dot(kt, q_ref[0, c * LANES:(c + 1) * LANES, :],
                                      preferred_element_type=F32))
        return scores

    def consume(j, scores, bias_idx):
        k0 = pl.multiple_of(j * t, t)
        for h in range(A_HEADS):
            v_aug = _with_ones(v_ref[0, h * A_V_DIM:(h + 1) * A_V_DIM, pl.ds(k0, t)])
            for mp in range(2):
                c = 2 * h + mp
                if bias_idx is None:
                    s2, shift = scores[c], cfar_ref[h]
                else:
                    s2, shift = bias_ref[h, bias_idx] + scores[c], 0.0
                _softmax_pv(s2, v_aug, m_ref.at[c], acc_ref.at[c], shift)

    _key_tile_loop(qt, qk, consume)

    lam_v = lam_ref[...]
    lam = (jnp.exp(jnp.sum(lam_v[0:1] * lam_v[1:2], axis=-1, keepdims=True))
           - jnp.exp(jnp.sum(lam_v[2:3] * lam_v[3:4], axis=-1, keepdims=True)) + lam_init)
    outs = []
    for h in range(A_HEADS):
        o = (_normalised(acc_ref[2 * h], A_V_DIM)
             - lam * _normalised(acc_ref[2 * h + 1], A_V_DIM))
        inv = lax.rsqrt(jnp.mean(o * o, axis=0, keepdims=True) + RMS_EPS)
        outs.append(o * inv * (g_ref[...] * (1.0 - lam_init)))
    o_ref[0] = jnp.concatenate(outs, axis=0).T.astype(o_ref.dtype)


def _attn_a(cfar, qa_t, ka, va_t, bias, lam_vecs, subln_col, lam_init):
    b, s, _ = ka.shape
    nq = s // ATT_TILE
    chains = 2 * A_HEADS
    return pl.pallas_call(
        functools.partial(_attn_a_kernel, lam_init=lam_init),
        grid=(b, nq),
        in_specs=[pl.BlockSpec(memory_space=pltpu.SMEM),
                  pl.BlockSpec((1, chains * LANES, ATT_TILE), lambda bi, qi: (bi, 0, qi)),
                  pl.BlockSpec((1, s, ka.shape[2]), lambda bi, qi: (bi, 0, 0)),
                  pl.BlockSpec((1, va_t.shape[1], s), lambda bi, qi: (bi, 0, 0)),
                  pl.BlockSpec(bias.shape, lambda bi, qi: (0, 0, 0, 0)),
                  pl.BlockSpec(lam_vecs.shape, lambda bi, qi: (0, 0)),
                  pl.BlockSpec(subln_col.shape, lambda bi, qi: (0, 0))],
        out_specs=pl.BlockSpec((1, ATT_TILE, A_HEADS * A_V_DIM), lambda bi, qi: (bi, qi, 0)),
        out_shape=jax.ShapeDtypeStruct((b, s, A_HEADS * A_V_DIM), BF16),
        scratch_shapes=[pltpu.VMEM((chains, 1, ATT_TILE), F32),
                        pltpu.VMEM((chains, A_V_DIM + BF16_ROWS, ATT_TILE), F32)],
        compiler_params=_cparams(("arbitrary", "arbitrary")),
    )(cfar, qa_t, ka, va_t, bias, lam_vecs, subln_col)


def _attn_b_kernel(cfar_ref, qi_ref, ki_ref, wi_ref, qb_ref, kb_ref, vb_ref, bias_ref, o_ref,
                   keys_ref, m_ref, acc_ref, *, top_k):
    qt = pl.program_id(1)
    n_tiles = qt + 1
    t = ATT_TILE
    chunk_ok = _chunk_visible_t((t, t))

    def tile_start(j):
        return pl.multiple_of(j * t, t)

    w = wi_ref[0]

    def score_tile(j, carry):
        k0 = tile_start(j)
        kt = ki_ref[0, pl.ds(k0, t), :]
        logits = [jnp.dot(kt, qi_ref[0, hh * IDX_DIM:(hh + 1) * IDX_DIM, :],
                          preferred_element_type=F32) for hh in range(IDX_HEADS)]
        score = None
        for hh in range(IDX_HEADS):
            term = w[hh:hh + 1, :] * jnp.maximum(logits[hh], 0.0)
            score = term if score is None else score + term
        score = jnp.where(chunk_ok | (j < qt), score, -jnp.inf)
        bits = lax.bitcast_convert_type(score, jnp.int32)
        keys_ref[pl.ds(k0, t), :] = jnp.where(bits < 0, (bits ^ 0x7FFFFFFF) + 1, bits)
        return carry

    lax.fori_loop(0, n_tiles, score_tile, 0)

    def count(pred_fn):
        def body(j, acc):
            hit = pred_fn(keys_ref[pl.ds(tile_start(j), t), :])
            return acc + jnp.sum(hit.reshape(t // COUNT_ROWS, COUNT_ROWS, t), axis=0)
        acc = lax.fori_loop(0, n_tiles, body, jnp.zeros((COUNT_ROWS, t), F32))
        return jnp.sum(acc, axis=0, keepdims=True)

    kf = float(top_k)

    def value_step(i, thr):
        cand = thr + lax.shift_left(jnp.int32(1), jnp.asarray(31 - i, jnp.int32))
        cnt = count(lambda blk: jnp.where(blk >= cand, 1.0, 0.0))
        return jnp.where(cnt >= kf, cand, thr)

    thr = lax.fori_loop(0, 32, value_step, jnp.full((1, t), INT_MIN, jnp.int32))
    need = kf - count(lambda blk: jnp.where(blk > thr, 1.0, 0.0))
    tri = jnp.where(lax.broadcasted_iota(jnp.int32, (t, t), 0)
                    >= lax.broadcasted_iota(jnp.int32, (t, t), 1), 1.0, 0.0).astype(BF16)

    def mask_tile(j, seen):
        k0 = tile_start(j)
        blk = keys_ref[pl.ds(k0, t), :]
        eq = jnp.where(blk == thr, 1.0, 0.0)
        rank = seen + jnp.dot(tri, eq.astype(BF16), preferred_element_type=F32)
        tie_keep = jnp.where(rank <= need, 0.0, NEG)
        mb = jnp.where(blk > thr, 0.0, jnp.where(blk == thr, tie_keep, NEG))
        keys_ref[pl.ds(k0, t), :] = lax.bitcast_convert_type(mb, jnp.int32)
        return rank[t - 1:t, :]

    lax.fori_loop(0, n_tiles, mask_tile, jnp.zeros((1, t), F32))

    _init_flash(m_ref, acc_ref)

    def qk(j):
        kt = kb_ref[0, pl.ds(tile_start(j), t), :]
        return [jnp.dot(kt, qb_ref[0, hh * B_HEAD_DIM:(hh + 1) * B_HEAD_DIM, :],
                        preferred_element_type=F32) for hh in range(B_HEADS)]

    def attend(j, scores, bias_idx):
        k0 = tile_start(j)
        v_aug = _with_ones(vb_ref[0, :, pl.ds(k0, t)])
        mb = lax.bitcast_convert_type(keys_ref[pl.ds(k0, t), :], F32)
        for hh in range(B_HEADS):
            if bias_idx is None:
                s2, shift = mb + scores[hh], cfar_ref[hh]
            else:
                s2, shift = (mb + bias_ref[hh, bias_idx]) + scores[hh], 0.0
            _softmax_pv(s2, v_aug, m_ref.at[hh], acc_ref.at[hh], shift)

    _key_tile_loop(qt, qk, attend)

    o_t = jnp.concatenate([_normalised(acc_ref[hh], B_HEAD_DIM) for hh in range(B_HEADS)], axis=0)
    o_ref[0] = o_t.T.astype(o_ref.dtype)


def _attn_b(cfar, qi_t, ki, wi_t, qb_t, kb, vb_t, bias, top_k):
    b, s, _ = kb.shape
    nq = s // ATT_TILE
    qmap = lambda bi, qt: (bi, 0, qt)
    kmap = lambda bi, qt: (bi, 0, 0)
    return pl.pallas_call(
        functools.partial(_attn_b_kernel, top_k=top_k),
        grid=(b, nq),
        in_specs=[pl.BlockSpec(memory_space=pltpu.SMEM),
                  pl.BlockSpec((1, qi_t.shape[1], ATT_TILE), qmap),
                  pl.BlockSpec((1, s, ki.shape[2]), kmap),
                  pl.BlockSpec((1, wi_t.shape[1], ATT_TILE), qmap),
                  pl.BlockSpec((1, qb_t.shape[1], ATT_TILE), qmap),
                  pl.BlockSpec((1, s, kb.shape[2]), kmap),
                  pl.BlockSpec((1, vb_t.shape[1], s), kmap),
                  pl.BlockSpec(bias.shape, lambda bi, qt: (0, 0, 0, 0))],
        out_specs=pl.BlockSpec((1, ATT_TILE, B_HEADS * B_HEAD_DIM), lambda bi, qt: (bi, qt, 0)),
        out_shape=jax.ShapeDtypeStruct((b, s, B_HEADS * B_HEAD_DIM), BF16),
        scratch_shapes=[pltpu.VMEM((s, ATT_TILE), jnp.int32),
                        pltpu.VMEM((B_HEADS, 1, ATT_TILE), F32),
                        pltpu.VMEM((B_HEADS, B_HEAD_DIM + BF16_ROWS, ATT_TILE), F32)],
        compiler_params=_cparams(("arbitrary", "arbitrary")),
    )(cfar, qi_t, ki, wi_t, qb_t, kb, vb_t, bias)


def _attn_c_kernel(q_ref, k_ref, v_ref, o_ref, m_ref, acc_ref):
    qt = pl.program_id(2)
    t = ATT_TILE
    chunk_ok = _chunk_visible_t((t, t))
    _init_flash(m_ref, acc_ref)

    def qk(j):
        k0 = pl.multiple_of(j * t, t)
        return [jnp.dot(k_ref[0, pl.ds(k0, t), g * LANES:(g + 1) * LANES],
                        q_ref[0, g * LANES:(g + 1) * LANES, :], preferred_element_type=F32)
                for g in range(C_GROUP)]

    def consume(j, scores, diagonal=False):
        k0 = pl.multiple_of(j * t, t)
        for g in range(C_GROUP):
            s2 = jnp.where(chunk_ok, scores[g], NEG) if diagonal else scores[g]
            v_aug = _with_ones(v_ref[0, g * C_V:(g + 1) * C_V, pl.ds(k0, t)])
            _softmax_pv(s2, v_aug, m_ref.at[g], acc_ref.at[g], 0.0)

    _far_tiles(qt, qk, consume)
    consume(qt, qk(qt), diagonal=True)
    o_t = jnp.concatenate([_normalised(acc_ref[g], C_V) for g in range(C_GROUP)], axis=0)
    o_ref[0] = o_t.T.astype(o_ref.dtype)


def _attn_c(q_t, k, v_t):
    b, s, _ = k.shape
    nq = s // ATT_TILE
    groups = C_HEADS // C_GROUP
    return pl.pallas_call(
        _attn_c_kernel,
        grid=(b, groups, nq),
        in_specs=[pl.BlockSpec((1, C_GROUP * LANES, ATT_TILE), lambda bi, p, qi: (bi, p, qi)),
                  pl.BlockSpec((1, s, C_GROUP * LANES), lambda bi, p, qi: (bi, 0, p)),
                  pl.BlockSpec((1, C_GROUP * C_V, s), lambda bi, p, qi: (bi, p, 0))],
        out_specs=pl.BlockSpec((1, ATT_TILE, C_GROUP * C_V), lambda bi, p, qi: (bi, qi, p)),
        out_shape=jax.ShapeDtypeStruct((b, s, C_HEADS * C_V), BF16),
        scratch_shapes=[pltpu.VMEM((C_GROUP, 1, ATT_TILE), F32),
                        pltpu.VMEM((C_GROUP, C_V + BF16_ROWS, ATT_TILE), F32)],
        compiler_params=_cparams(("arbitrary", "arbitrary", "arbitrary")),
    )(q_t, k, v_t)


def _rot_cols(w):
    half = w.shape[-1] // 2
    return jnp.concatenate([-w[..., half:], w[..., :half]], axis=-1)


def _rope_slot(w_rope):
    return jnp.pad(w_rope, ((0, 0), (0, 0), (C_NOPE, LANES - C_NOPE - C_ROPE)))


def _t_bf16(w):
    return w.T.astype(BF16)


def kernel(x, rel_bias, e_norm, e_w_in, e_lam_q1, e_lam_k1, e_lam_q2, e_lam_k2, e_subln, e_w_o,
           o_norm, o_w_in, o_q_norm, o_w_uq, o_kv_norm, o_w_ukv, o_w_o, final_norm):
    b, s, d = x.shape
    assert s % ROW_TILE == 0 and s % ATT_TILE == 0 and ATT_TILE % CHUNK == 0
    assert ATT_TILE >= MAX_DISTANCE, "key tiles two or more before the diagonal must be all-far"
    top_k = min(TOPK_MAX, s // 4)
    assert top_k <= ATT_TILE, "every query must see at least top_k keys inside its own window"
    n = b * s

    bias = _bias_tiles(rel_bias)
    cfar = rel_bias[T5_FAR_BUCKET] * LOG2E

    w_in = e_w_in[0]
    sizes = (512, 512, 512, 512, 64, 64, 512, 64, 8, 1024)
    offs = np.concatenate([[0], np.cumsum(sizes)])
    col = lambda i: w_in[:, offs[i]:offs[i + 1]]
    zeros64 = jnp.zeros((d, LANES - B_HEAD_DIM), F32)
    w_tok = jnp.concatenate([col(1), col(4), zeros64, col(7), zeros64], axis=1).astype(BF16)
    wqa = col(0).reshape(d, A_HEADS, 2, A_HEAD_DIM)
    zq = jnp.zeros((d, A_HEADS, A_HEAD_DIM), F32)
    wqa = jnp.stack([jnp.concatenate([wqa[:, :, 0], zq], axis=-1),
                     jnp.concatenate([zq, wqa[:, :, 1]], axis=-1)], axis=2)
    wqa_t = _t_bf16(wqa.reshape(d, 2 * A_HEADS * LANES))
    wwi_t = _t_bf16(jnp.pad(col(8), ((0, 0), (0, BF16_ROWS - IDX_HEADS))))
    ka, kb, ki, gate0, qa_t, va_t, qb_t, qi_t, vb_t, wi_t = _proj0(
        x, e_norm[0][None], w_tok, col(9).astype(BF16), wqa_t, _t_bf16(col(2)), _t_bf16(col(3)),
        _t_bf16(col(6)), _t_bf16(col(5)), wwi_t, (A_HEAD_DIM ** -0.5) * LOG2E)

    lam_init = 0.8 - 0.6 * math.exp(-0.3 * 0)
    lam_vecs = jnp.stack([e_lam_q1[0], e_lam_k1[0], e_lam_q2[0], e_lam_k2[0]])
    o_a = _attn_a(cfar[:A_HEADS], qa_t, ka, va_t, bias[:A_HEADS], lam_vecs,
                  e_subln[0][:, None], lam_init)
    o_b = _attn_b(cfar[A_HEADS:], qi_t, ki, wi_t, qb_t, kb, vb_t, bias[A_HEADS:], top_k)
    x2 = x.reshape(n, d)
    h1 = _out_proj([o_a.reshape(n, -1), o_b.reshape(n, -1)], gate0.reshape(n, -1), x2,
                   e_w_o[0].astype(BF16))

    w_in1 = o_w_in[0]
    w_ql = w_in1[:, :Q_LORA]
    w_kvl = w_in1[:, Q_LORA:Q_LORA + KV_LORA]
    w_kr = w_in1[:, Q_LORA + KV_LORA:Q_LORA + KV_LORA + C_ROPE]
    w_gate1 = w_in1[:, Q_LORA + KV_LORA + C_ROPE:].astype(BF16)
    kr_slot = _rope_slot(w_kr[:, None, :])[:, 0]
    kr_rot_slot = _rope_slot(_rot_cols(w_kr)[:, None, :])[:, 0]
    w_lat = jnp.concatenate([w_ql, w_kvl, kr_slot, kr_rot_slot], axis=1).astype(BF16)

    w_uq = o_w_uq[0].reshape(Q_LORA, C_HEADS, C_NOPE + C_ROPE)
    w_q = jnp.pad(w_uq[..., :C_NOPE], ((0, 0), (0, 0), (0, LANES - C_NOPE))) \
        + _rope_slot(w_uq[..., C_NOPE:])
    w_qrot = _rope_slot(_rot_cols(w_uq[..., C_NOPE:]))
    w_ukv = o_w_ukv[0].reshape(KV_LORA, C_HEADS, C_NOPE + C_V)
    w_k = jnp.pad(w_ukv[..., :C_NOPE], ((0, 0), (0, 0), (0, LANES - C_NOPE)))
    w_v = w_ukv[..., C_NOPE:]
    flat = lambda w: w.reshape(w.shape[0], -1)

    inv = ROPE_THETA ** (-jnp.arange(0, C_ROPE, 2, dtype=F32) / C_ROPE)
    ang = jnp.arange(s, dtype=F32)[:, None] * inv[None, :]
    cos, sin = jnp.cos(ang), jnp.sin(ang)
    pad_hi = jnp.zeros((s, LANES - C_NOPE - C_ROPE), F32)
    cos_tab = jnp.concatenate([jnp.ones((s, C_NOPE), F32), cos, cos, pad_hi], axis=1)
    sin_tab = jnp.concatenate([jnp.zeros((s, C_NOPE), F32), sin, sin, pad_hi], axis=1)

    q1_t, k1, v1_t, gate1 = _proj1(
        h1.reshape(b, s, d), o_norm[0][None], cos_tab, sin_tab, cos_tab.T, sin_tab.T, w_lat,
        w_gate1, o_q_norm[0][None], o_kv_norm[0][None], _t_bf16(flat(w_q)), _t_bf16(flat(w_qrot)),
        flat(w_k).astype(BF16), _t_bf16(flat(w_v)), ((C_NOPE + C_ROPE) ** -0.5) * LOG2E)
    o_c = _attn_c(q1_t, k1, v1_t)
    out = _out_proj([o_c.reshape(n, -1)], gate1.reshape(n, -1), h1, o_w_o[0].astype(BF16),
                    final_norm[None])
    return out.reshape(b, s, d)
```

```python
import functools
import math

import numpy as np
import jax
import jax.numpy as jnp
from jax import lax
from jax.experimental import pallas as pl
from jax.experimental.pallas import tpu as pltpu

F32 = jnp.float32
BF16 = jnp.bfloat16

CHUNK = 64
RMS_EPS = 1e-6
N_BUCKETS = 32
MAX_DISTANCE = 128
A_HEADS = 4
A_HEAD_DIM = 64
A_V_DIM = 2 * A_HEAD_DIM
B_HEADS = 8
B_HEAD_DIM = 64
IDX_HEADS = 8
IDX_DIM = 64
TOPK_MAX = 256
C_HEADS = 16
C_NOPE = 64
C_ROPE = 32
C_V = 64
Q_LORA = 384
KV_LORA = 256
ROPE_THETA = 10000.0

LANES = 128
BF16_ROWS = 16
ATT_TILE = 256
ROW_TILE = 512
VMEM_LIMIT = 56 * 1024 * 1024

C_GROUP = 8
COUNT_ROWS = 32
LOG2E = math.log2(math.e)
NEG = -1e30
INT_MIN = -(2 ** 31)
NT_DIMS = (((1,), (1,)), ((), ()))
CHUNK_SHIFT = CHUNK.bit_length() - 1
assert 1 << CHUNK_SHIFT == CHUNK


def _t5_thresholds():
    nb = N_BUCKETS // 2
    max_exact = nb // 2
    n = np.arange(max_exact, MAX_DISTANCE + 1, dtype=np.float64)
    large = max_exact + (np.log(n / max_exact) / math.log(MAX_DISTANCE / max_exact)
                         * (nb - max_exact)).astype(np.int64)
    large = np.minimum(large, nb - 1)
    thr = [int(n[np.argmax(large >= b)]) for b in range(max_exact + 1, nb)]
    return max_exact, thr


T5_MAX_EXACT, T5_THRESHOLDS = _t5_thresholds()
T5_FAR_BUCKET = N_BUCKETS // 2 - 1


def _cparams(sem):
    return pltpu.CompilerParams(dimension_semantics=sem, vmem_limit_bytes=VMEM_LIMIT)


def _rms(x, g):
    return x * lax.rsqrt(jnp.mean(x * x, axis=-1, keepdims=True) + RMS_EPS) * g


def _silu(g):
    return g / (1.0 + jnp.exp(-g))


def _chunk_visible_t(shape):
    key = lax.broadcasted_iota(jnp.int32, shape, 0)
    qry = lax.broadcasted_iota(jnp.int32, shape, 1)
    return (key >> CHUNK_SHIFT) <= (qry >> CHUNK_SHIFT)


def _nt(w_t, x):
    return lax.dot_general(w_t, x, NT_DIMS, preferred_element_type=F32)


def _bias_tile_kernel(tab_ref, o_ref):
    h = pl.program_id(0)
    t = pl.program_id(1)
    shape = (ATT_TILE, ATT_TILE)
    key = lax.broadcasted_iota(jnp.int32, shape, 0)
    qry = lax.broadcasted_iota(jnp.int32, shape, 1)
    rel = key - qry - t * ATT_TILE
    n = jnp.abs(rel)
    large = jnp.full(shape, T5_MAX_EXACT, jnp.int32)
    for thr in T5_THRESHOLDS:
        large = large + jnp.where(n >= thr, 1, 0)
    bucket = jnp.where(rel > 0, N_BUCKETS // 2, 0) + jnp.where(n < T5_MAX_EXACT, n, large)
    val = jnp.full(shape, tab_ref[0, h], F32)
    for b in range(1, N_BUCKETS):
        val = jnp.where(bucket == b, tab_ref[b, h], val)
    visible = _chunk_visible_t(shape) | (t > 0)
    o_ref[0, 0] = jnp.where(visible, val * LOG2E, NEG)


def _bias_tiles(rel_bias):
    heads = rel_bias.shape[1]
    return pl.pallas_call(
        _bias_tile_kernel,
        grid=(heads, 2),
        in_specs=[pl.BlockSpec(memory_space=pltpu.SMEM)],
        out_specs=pl.BlockSpec((1, 1, ATT_TILE, ATT_TILE), lambda h, t: (h, t, 0, 0)),
        out_shape=jax.ShapeDtypeStruct((heads, 2, ATT_TILE, ATT_TILE), F32),
        compiler_params=_cparams(("arbitrary", "arbitrary")),
    )(rel_bias)


def _proj0_kernel(x_ref, g_ref, w_tok_ref, w_feat_ref, ka_ref, kb_ref, ki_ref, gate_ref,
                  qa_ref, va_ref, qb_ref, qi_ref, vb_ref, wi_ref, *, q_scale):
    hn = _rms(x_ref[0], g_ref[...]).astype(BF16)
    tok = jnp.dot(hn, w_tok_ref[...], preferred_element_type=F32)
    ka_ref[0] = tok[:, :512].astype(BF16)
    kb_ref[0] = tok[:, 512:512 + B_HEAD_DIM].astype(BF16)
    ki_ref[0] = tok[:, 512 + LANES:512 + LANES + IDX_DIM].astype(BF16)
    gate_ref[0] = tok[:, 512 + 2 * LANES:].astype(BF16)
    feat = _nt(w_feat_ref[...], hn)
    off = 0
    for o_ref, scale in ((qa_ref, q_scale), (va_ref, None), (qb_ref, q_scale), (qi_ref, None),
                         (vb_ref, None), (wi_ref, None)):
        rows = o_ref.shape[1]
        y = feat[off:off + rows]
        o_ref[0] = (y if scale is None else y * scale).astype(o_ref.dtype)
        off += rows


def _proj0(x, g, w_tok, w_feat_t, feat_rows, gate_cols, q_scale):
    b, s, d = x.shape
    tok = lambda c, dt: (pl.BlockSpec((1, ROW_TILE, c), lambda bi, i: (bi, i, 0)),
                         jax.ShapeDtypeStruct((b, s, c), dt))
    feat = lambda r, dt: (pl.BlockSpec((1, r, ROW_TILE), lambda bi, i: (bi, 0, i)),
                          jax.ShapeDtypeStruct((b, r, s), dt))
    full = lambda a: pl.BlockSpec(a.shape, lambda bi, i: (0, 0))
    assert sum(feat_rows) == w_feat_t.shape[0]
    outs = [tok(512, BF16), tok(B_HEAD_DIM, BF16), tok(IDX_DIM, BF16), tok(gate_cols, BF16)]
    outs += [feat(r, BF16) for r in feat_rows[:-1]] + [feat(feat_rows[-1], F32)]
    return pl.pallas_call(
        functools.partial(_proj0_kernel, q_scale=q_scale),
        grid=(b, s // ROW_TILE),
        in_specs=[pl.BlockSpec((1, ROW_TILE, d), lambda bi, i: (bi, i, 0)), full(g),
                  full(w_tok), full(w_feat_t)],
        out_specs=[o[0] for o in outs],
        out_shape=[o[1] for o in outs],
        compiler_params=_cparams(("arbitrary", "arbitrary")),
    )(x, g, w_tok, w_feat_t)


def _proj1_kernel(x_ref, g_ref, cos_ref, sin_ref, cos_t_ref, sin_t_ref, w_tok_ref,
                  qn_ref, kvn_ref, wqq_ref, wk_ref, wv_ref,
                  q_ref, k_ref, v_ref, gate_ref, *, q_scale):
    hn = _rms(x_ref[0], g_ref[...]).astype(BF16)
    tok = jnp.dot(hn, w_tok_ref[...], preferred_element_type=F32)
    q_lat = tok[:, :Q_LORA]
    kv_lat = tok[:, Q_LORA:Q_LORA + KV_LORA]
    kr = tok[:, Q_LORA + KV_LORA:Q_LORA + KV_LORA + LANES]
    kr_rot = tok[:, Q_LORA + KV_LORA + LANES:Q_LORA + KV_LORA + 2 * LANES]
    gate_ref[0] = tok[:, Q_LORA + KV_LORA + 2 * LANES:].astype(BF16)
    k_rope = kr * cos_ref[...] + kr_rot * sin_ref[...]
    qn = _rms(q_lat, qn_ref[...]).astype(BF16)
    kvn = _rms(kv_lat, kvn_ref[...]).astype(BF16)
    cos_t = cos_t_ref[...] * q_scale
    sin_t = sin_t_ref[...] * q_scale
    qq = _nt(wqq_ref[...], qn)
    kn = jnp.dot(kvn, wk_ref[...], preferred_element_type=F32)
    for h in range(C_HEADS):
        sl = slice(h * LANES, (h + 1) * LANES)
        rot = slice((C_HEADS + h) * LANES, (C_HEADS + h + 1) * LANES)
        q_ref[0, sl, :] = (qq[sl] * cos_t + qq[rot] * sin_t).astype(BF16)
        k_ref[0, :, sl] = (kn[:, sl] + k_rope).astype(BF16)
    v_ref[0] = _nt(wv_ref[...], kvn).astype(BF16)


def _proj1(x, g, cos, sin, cos_t, sin_t, w_tok, gate_cols, qn_g, kvn_g, wqq_t, w_k, wv_t, q_scale):
    b, s, d = x.shape
    tok = lambda c, dt: (pl.BlockSpec((1, ROW_TILE, c), lambda bi, i: (bi, i, 0)),
                         jax.ShapeDtypeStruct((b, s, c), dt))
    feat = lambda r, dt: (pl.BlockSpec((1, r, ROW_TILE), lambda bi, i: (bi, 0, i)),
                          jax.ShapeDtypeStruct((b, r, s), dt))
    full = lambda a: pl.BlockSpec(a.shape, lambda bi, i: (0, 0))
    outs = [feat(C_HEADS * LANES, BF16), tok(C_HEADS * LANES, BF16), feat(C_HEADS * C_V, BF16),
            tok(gate_cols, BF16)]
    return pl.pallas_call(
        functools.partial(_proj1_kernel, q_scale=q_scale),
        grid=(b, s // ROW_TILE),
        in_specs=[pl.BlockSpec((1, ROW_TILE, d), lambda bi, i: (bi, i, 0)), full(g),
                  pl.BlockSpec((ROW_TILE, LANES), lambda bi, i: (i, 0)),
                  pl.BlockSpec((ROW_TILE, LANES), lambda bi, i: (i, 0)),
                  pl.BlockSpec((LANES, ROW_TILE), lambda bi, i: (0, i)),
                  pl.BlockSpec((LANES, ROW_TILE), lambda bi, i: (0, i)),
                  full(w_tok), full(qn_g), full(kvn_g), full(wqq_t), full(w_k), full(wv_t)],
        out_specs=[o[0] for o in outs],
        out_shape=[o[1] for o in outs],
        compiler_params=_cparams(("arbitrary", "arbitrary")),
    )(x, g, cos, sin, cos_t, sin_t, w_tok, qn_g, kvn_g, wqq_t, w_k, wv_t)


def _out_kernel(*refs, n_mix, final_norm):
    mix_refs = refs[:n_mix]
    gate_ref, x_ref, w_ref = refs[n_mix:n_mix + 3]
    rest = refs[n_mix + 3:]
    sg = _silu(gate_ref[...].astype(F32))
    y = x_ref[...]
    off = 0
    for m_ref in mix_refs:
        c = m_ref.shape[1]
        a = (m_ref[...].astype(F32) * sg[:, off:off + c]).astype(BF16)
        y = y + jnp.dot(a, w_ref[off:off + c, :], preferred_element_type=F32)
        off += c
    if final_norm:
        g_ref, o_ref = rest
        o_ref[...] = _rms(y, g_ref[...])
    else:
        (o_ref,) = rest
        o_ref[...] = y


def _out_proj(mixes, gate, x2, w_o, final_g=None):
    n, d = x2.shape
    rows = lambda i: (i, 0)
    const = lambda i: (0, 0)
    in_specs = [pl.BlockSpec((ROW_TILE, m.shape[1]), rows) for m in mixes]
    in_specs += [pl.BlockSpec((ROW_TILE, gate.shape[1]), rows), pl.BlockSpec((ROW_TILE, d), rows),
                 pl.BlockSpec(w_o.shape, const)]
    args = list(mixes) + [gate, x2, w_o]
    if final_g is not None:
        in_specs.append(pl.BlockSpec((1, d), const))
        args.append(final_g)
    return pl.pallas_call(
        functools.partial(_out_kernel, n_mix=len(mixes), final_norm=final_g is not None),
        grid=(n // ROW_TILE,),
        in_specs=in_specs,
        out_specs=pl.BlockSpec((ROW_TILE, d), rows),
        out_shape=jax.ShapeDtypeStruct((n, d), F32),
        compiler_params=_cparams(("arbitrary",)),
    )(*args)


def _with_ones(v_t):
    return jnp.concatenate([v_t, jnp.ones((BF16_ROWS, v_t.shape[1]), v_t.dtype)], axis=0)


def _softmax_pv(s2, v_aug, m_ref, acc_ref, shift):
    m_old = m_ref[...]
    m_new = jnp.maximum(m_old, jnp.max(s2, axis=0, keepdims=True) + shift)
    p = jnp.exp2(s2 - (m_new - shift))
    alpha = jnp.exp2(m_old - m_new)
    acc_ref[...] = alpha * acc_ref[...] + jnp.dot(v_aug, p.astype(v_aug.dtype),
                                                  preferred_element_type=F32)
    m_ref[...] = m_new


def _init_flash(m_ref, acc_ref):
    m_ref[...] = jnp.full(m_ref.shape, -jnp.inf, F32)
    acc_ref[...] = jnp.zeros(acc_ref.shape, F32)


def _normalised(acc, dv):
    return acc[:dv] / acc[dv:dv + 1]


def _far_pairs(n_far, qk_fn, consume_fn):
    def pair_body(i, carry):
        ja, jb = 2 * i, 2 * i + 1
        sa = qk_fn(ja)
        sb = qk_fn(jb)
        consume_fn(ja, sa)
        consume_fn(jb, sb)
        return carry
    lax.fori_loop(0, n_far >> 1, pair_body, 0)


def _key_tile_loop(qt, qk_fn, consume_fn):
    n_far = jnp.maximum(qt - 1, 0)
    _far_pairs(n_far, qk_fn, lambda j, sc: consume_fn(j, sc, None))

    @pl.when((n_far & 1) == 1)
    def _():
        consume_fn(n_far - 1, qk_fn(n_far - 1), None)

    @pl.when(qt >= 1)
    def _():
        s_near = qk_fn(qt - 1)
        s_diag = qk_fn(qt)
        consume_fn(qt - 1, s_near, 1)
        consume_fn(qt, s_diag, 0)

    @pl.when(qt == 0)
    def _():
        consume_fn(qt, qk_fn(qt), 0)


def _attn_a_kernel(cfar_ref, q_ref, k_ref, v_ref, bias_ref, lam_ref, g_ref, o_ref,
                   m_ref, acc_ref, *, lam_init):
    qt = pl.program_id(1)
    t = ATT_TILE
    _init_flash(m_ref, acc_ref)

    def qk(j):
        k0 = pl.multiple_of(j * t, t)
        scores = []
        for h in range(A_HEADS):
            kt = k_ref[0, pl.ds(k0, t), h * LANES:(h + 1) * LANES]
            for mp in range(2):
                c = 2 * h + mp
                scores.append(jnp.dot(kt, q_ref[0, c * LANES:(c + 1) * LANES, :],
                                      preferred_element_type=F32))
        return scores

    def consume(j, scores, bias_idx):
        k0 = pl.multiple_of(j * t, t)
        for h in range(A_HEADS):
            v_aug = _with_ones(v_ref[0, h * A_V_DIM:(h + 1) * A_V_DIM, pl.ds(k0, t)])
            for mp in range(2):
                c = 2 * h + mp
                if bias_idx is None:
                    s2, shift = scores[c], cfar_ref[h]
                else:
                    s2, shift = bias_ref[h, bias_idx] + scores[c], 0.0
                _softmax_pv(s2, v_aug, m_ref.at[c], acc_ref.at[c], shift)

    _key_tile_loop(qt, qk, consume)

    lam_v = lam_ref[...]
    lam = (jnp.exp(jnp.sum(lam_v[0:1] * lam_v[1:2], axis=-1, keepdims=True))
           - jnp.exp(jnp.sum(lam_v[2:3] * lam_v[3:4], axis=-1, keepdims=True)) + lam_init)
    outs = []
    for h in range(A_HEADS):
        o = (_normalised(acc_ref[2 * h], A_V_DIM)
             - lam * _normalised(acc_ref[2 * h + 1], A_V_DIM))
        inv = lax.rsqrt(jnp.mean(o * o, axis=0, keepdims=True) + RMS_EPS)
        outs.append(o * inv * (g_ref[...] * (1.0 - lam_init)))
    o_ref[0] = jnp.concatenate(outs, axis=0).T.astype(o_ref.dtype)


def _attn_a(cfar, qa_t, ka, va_t, bias, lam_vecs, subln_col, lam_init):
    b, s, _ = ka.shape
    nq = s // ATT_TILE
    chains = 2 * A_HEADS
    return pl.pallas_call(
        functools.partial(_attn_a_kernel, lam_init=lam_init),
        grid=(b, nq),
        in_specs=[pl.BlockSpec(memory_space=pltpu.SMEM),
                  pl.BlockSpec((1, chains * LANES, ATT_TILE), lambda bi, qi: (bi, 0, qi)),
                  pl.BlockSpec((1, s, ka.shape[2]), lambda bi, qi: (bi, 0, 0)),
                  pl.BlockSpec((1, va_t.shape[1], s), lambda bi, qi: (bi, 0, 0)),
                  pl.BlockSpec(bias.shape, lambda bi, qi: (0, 0, 0, 0)),
                  pl.BlockSpec(lam_vecs.shape, lambda bi, qi: (0, 0)),
                  pl.BlockSpec(subln_col.shape, lambda bi, qi: (0, 0))],
        out_specs=pl.BlockSpec((1, ATT_TILE, A_HEADS * A_V_DIM), lambda bi, qi: (bi, qi, 0)),
        out_shape=jax.ShapeDtypeStruct((b, s, A_HEADS * A_V_DIM), BF16),
        scratch_shapes=[pltpu.VMEM((chains, 1, ATT_TILE), F32),
                        pltpu.VMEM((chains, A_V_DIM + BF16_ROWS, ATT_TILE), F32)],
        compiler_params=_cparams(("arbitrary", "arbitrary")),
    )(cfar, qa_t, ka, va_t, bias, lam_vecs, subln_col)


def _attn_b_kernel(cfar_ref, qi_ref, ki_ref, wi_ref, qb_ref, kb_ref, vb_ref, bias_ref, o_ref,
                   keys_ref, seen_ref, m_ref, acc_ref, *, top_k):
    qt = pl.program_id(1)
    n_tiles = qt + 1
    t = ATT_TILE
    chunk_ok = _chunk_visible_t((t, t))

    def tile_start(j):
        return pl.multiple_of(j * t, t)

    w = wi_ref[0]

    def score_tile(j, carry):
        k0 = tile_start(j)
        kt = ki_ref[0, pl.ds(k0, t), :]
        logits = [jnp.dot(kt, qi_ref[0, hh * IDX_DIM:(hh + 1) * IDX_DIM, :],
                          preferred_element_type=F32) for hh in range(IDX_HEADS)]
        score = None
        for hh in range(IDX_HEADS):
            term = w[hh:hh + 1, :] * jnp.maximum(logits[hh], 0.0)
            score = term if score is None else score + term
        score = jnp.where(chunk_ok | (j < qt), score, -jnp.inf)
        bits = lax.bitcast_convert_type(score, jnp.int32)
        keys_ref[pl.ds(k0, t), :] = jnp.where(bits < 0, (bits ^ 0x7FFFFFFF) + 1, bits)
        return carry

    def score_pair(i, carry):
        score_tile(2 * i, carry)
        return score_tile(2 * i + 1, carry)

    lax.fori_loop(0, n_tiles >> 1, score_pair, 0)

    @pl.when((n_tiles & 1) == 1)
    def _():
        score_tile(n_tiles - 1, 0)

    def count(pred_fn):
        def body(j, acc):
            hit = pred_fn(keys_ref[pl.ds(tile_start(j), t), :])
            return acc + jnp.sum(hit.reshape(t // COUNT_ROWS, COUNT_ROWS, t), axis=0)
        acc = lax.fori_loop(0, n_tiles, body, jnp.zeros((COUNT_ROWS, t), F32))
        return jnp.sum(acc, axis=0, keepdims=True)

    kf = float(top_k)

    def value_step(i, thr):
        cand = thr + lax.shift_left(jnp.int32(1), jnp.asarray(31 - i, jnp.int32))
        cnt = count(lambda blk: jnp.where(blk >= cand, 1.0, 0.0))
        return jnp.where(cnt >= kf, cand, thr)

    thr = lax.fori_loop(0, 32, value_step, jnp.full((1, t), INT_MIN, jnp.int32))
    need = kf - count(lambda blk: jnp.where(blk > thr, 1.0, 0.0))
    tri = jnp.where(lax.broadcasted_iota(jnp.int32, (t, t), 0)
                    >= lax.broadcasted_iota(jnp.int32, (t, t), 1), 1.0, 0.0).astype(BF16)

    _init_flash(m_ref, acc_ref)
    seen_ref[...] = jnp.zeros(seen_ref.shape, F32)

    def qk(j):
        k0 = tile_start(j)
        kt = kb_ref[0, pl.ds(k0, t), :]
        scores = [jnp.dot(kt, qb_ref[0, hh * B_HEAD_DIM:(hh + 1) * B_HEAD_DIM, :],
                          preferred_element_type=F32) for hh in range(B_HEADS)]
        eq = jnp.where(keys_ref[pl.ds(k0, t), :] == thr, 1.0, 0.0)
        scores.append(jnp.dot(tri, eq.astype(BF16), preferred_element_type=F32))
        return scores

    def attend(j, scores, bias_idx):
        k0 = tile_start(j)
        v_aug = _with_ones(vb_ref[0, :, pl.ds(k0, t)])
        blk = keys_ref[pl.ds(k0, t), :]
        rank = seen_ref[...] + scores[B_HEADS]
        seen_ref[...] = rank[t - 1:t, :]
        tie_keep = jnp.where(rank <= need, 0.0, NEG)
        mb = jnp.where(blk > thr, 0.0, jnp.where(blk == thr, tie_keep, NEG))
        for hh in range(B_HEADS):
            if bias_idx is None:
                s2, shift = mb + scores[hh], cfar_ref[hh]
            else:
                s2, shift = (mb + bias_ref[hh, bias_idx]) + scores[hh], 0.0
            _softmax_pv(s2, v_aug, m_ref.at[hh], acc_ref.at[hh], shift)

    _key_tile_loop(qt, qk, attend)

    o_t = jnp.concatenate([_normalised(acc_ref[hh], B_HEAD_DIM) for hh in range(B_HEADS)], axis=0)
    o_ref[0] = o_t.T.astype(o_ref.dtype)


def _attn_b(cfar, qi_t, ki, wi_t, qb_t, kb, vb_t, bias, top_k):
    b, s, _ = kb.shape
    nq = s // ATT_TILE
    qmap = lambda bi, qt: (bi, 0, qt)
    kmap = lambda bi, qt: (bi, 0, 0)
    return pl.pallas_call(
        functools.partial(_attn_b_kernel, top_k=top_k),
        grid=(b, nq),
        in_specs=[pl.BlockSpec(memory_space=pltpu.SMEM),
                  pl.BlockSpec((1, qi_t.shape[1], ATT_TILE), qmap),
                  pl.BlockSpec((1, s, ki.shape[2]), kmap),
                  pl.BlockSpec((1, wi_t.shape[1], ATT_TILE), qmap),
                  pl.BlockSpec((1, qb_t.shape[1], ATT_TILE), qmap),
                  pl.BlockSpec((1, s, kb.shape[2]), kmap),
                  pl.BlockSpec((1, vb_t.shape[1], s), kmap),
                  pl.BlockSpec(bias.shape, lambda bi, qt: (0, 0, 0, 0))],
        out_specs=pl.BlockSpec((1, ATT_TILE, B_HEADS * B_HEAD_DIM), lambda bi, qt: (bi, qt, 0)),
        out_shape=jax.ShapeDtypeStruct((b, s, B_HEADS * B_HEAD_DIM), BF16),
        scratch_shapes=[pltpu.VMEM((s, ATT_TILE), jnp.int32),
                        pltpu.VMEM((1, ATT_TILE), F32),
                        pltpu.VMEM((B_HEADS, 1, ATT_TILE), F32),
                        pltpu.VMEM((B_HEADS, B_HEAD_DIM + BF16_ROWS, ATT_TILE), F32)],
        compiler_params=_cparams(("arbitrary", "arbitrary")),
    )(cfar, qi_t, ki, wi_t, qb_t, kb, vb_t, bias)


def _attn_c_kernel(q_ref, k_ref, v_ref, o_ref, m_ref, acc_ref):
    qt = pl.program_id(2)
    t = ATT_TILE
    chunk_ok = _chunk_visible_t((t, t))
    _init_flash(m_ref, acc_ref)

    def qk(j):
        k0 = pl.multiple_of(j * t, t)
        return [jnp.dot(k_ref[0, pl.ds(k0, t), g * LANES:(g + 1) * LANES],
                        q_ref[0, g * LANES:(g + 1) * LANES, :], preferred_element_type=F32)
                for g in range(C_GROUP)]

    def consume(j, scores, diagonal=False):
        k0 = pl.multiple_of(j * t, t)
        for g in range(C_GROUP):
            s2 = jnp.where(chunk_ok, scores[g], NEG) if diagonal else scores[g]
            v_aug = _with_ones(v_ref[0, g * C_V:(g + 1) * C_V, pl.ds(k0, t)])
            _softmax_pv(s2, v_aug, m_ref.at[g], acc_ref.at[g], 0.0)

    _far_pairs(qt, qk, consume)

    @pl.when((qt & 1) == 1)
    def _():
        s_far = qk(qt - 1)
        s_diag = qk(qt)
        consume(qt - 1, s_far)
        consume(qt, s_diag, diagonal=True)

    @pl.when((qt & 1) == 0)
    def _():
        consume(qt, qk(qt), diagonal=True)
    o_t = jnp.concatenate([_normalised(acc_ref[g], C_V) for g in range(C_GROUP)], axis=0)
    o_ref[0] = o_t.T.astype(o_ref.dtype)


def _attn_c(q_t, k, v_t):
    b, s, _ = k.shape
    nq = s // ATT_TILE
    groups = C_HEADS // C_GROUP
    return pl.pallas_call(
        _attn_c_kernel,
        grid=(b, groups, nq),
        in_specs=[pl.BlockSpec((1, C_GROUP * LANES, ATT_TILE), lambda bi, p, qi: (bi, p, qi)),
                  pl.BlockSpec((1, s, C_GROUP * LANES), lambda bi, p, qi: (bi, 0, p)),
                  pl.BlockSpec((1, C_GROUP * C_V, s), lambda bi, p, qi: (bi, p, 0))],
        out_specs=pl.BlockSpec((1, ATT_TILE, C_GROUP * C_V), lambda bi, p, qi: (bi, qi, p)),
        out_shape=jax.ShapeDtypeStruct((b, s, C_HEADS * C_V), BF16),
        scratch_shapes=[pltpu.VMEM((C_GROUP, 1, ATT_TILE), F32),
                        pltpu.VMEM((C_GROUP, C_V + BF16_ROWS, ATT_TILE), F32)],
        compiler_params=_cparams(("arbitrary", "arbitrary", "arbitrary")),
    )(q_t, k, v_t)


def _rot_cols(w):
    half = w.shape[-1] // 2
    return jnp.concatenate([-w[..., half:], w[..., :half]], axis=-1)


def _rope_slot(w_rope):
    return jnp.pad(w_rope, ((0, 0), (0, 0), (C_NOPE, LANES - C_NOPE - C_ROPE)))


def _t_bf16(w):
    return w.T.astype(BF16)


def kernel(x, rel_bias, e_norm, e_w_in, e_lam_q1, e_lam_k1, e_lam_q2, e_lam_k2, e_subln, e_w_o,
           o_norm, o_w_in, o_q_norm, o_w_uq, o_kv_norm, o_w_ukv, o_w_o, final_norm):
    b, s, d = x.shape
    assert s % ROW_TILE == 0 and s % ATT_TILE == 0 and ATT_TILE % CHUNK == 0
    assert ATT_TILE >= MAX_DISTANCE, "key tiles two or more before the diagonal must be all-far"
    top_k = min(TOPK_MAX, s // 4)
    assert top_k <= ATT_TILE, "every query must see at least top_k keys inside its own window"
    n = b * s

    bias = _bias_tiles(rel_bias)
    cfar = rel_bias[T5_FAR_BUCKET] * LOG2E

    w_in = e_w_in[0]
    sizes = (512, 512, 512, 512, 64, 64, 512, 64, 8, 1024)
    offs = np.concatenate([[0], np.cumsum(sizes)])
    col = lambda i: w_in[:, offs[i]:offs[i + 1]]
    zeros64 = jnp.zeros((d, LANES - B_HEAD_DIM), F32)
    w_tok = jnp.concatenate([col(1), col(4), zeros64, col(7), zeros64, col(9)],
                            axis=1).astype(BF16)
    wqa = col(0).reshape(d, A_HEADS, 2, A_HEAD_DIM)
    zq = jnp.zeros((d, A_HEADS, A_HEAD_DIM), F32)
    wqa = jnp.stack([jnp.concatenate([wqa[:, :, 0], zq], axis=-1),
                     jnp.concatenate([zq, wqa[:, :, 1]], axis=-1)], axis=2)
    feat_w = [wqa.reshape(d, 2 * A_HEADS * LANES), col(2), col(3), col(6), col(5),
              jnp.pad(col(8), ((0, 0), (0, BF16_ROWS - IDX_HEADS)))]
    ka, kb, ki, gate0, qa_t, va_t, qb_t, qi_t, vb_t, wi_t = _proj0(
        x, e_norm[0][None], w_tok, _t_bf16(jnp.concatenate(feat_w, axis=1)),
        [w.shape[1] for w in feat_w], sizes[9], (A_HEAD_DIM ** -0.5) * LOG2E)

    lam_init = 0.8 - 0.6 * math.exp(-0.3 * 0)
    lam_vecs = jnp.stack([e_lam_q1[0], e_lam_k1[0], e_lam_q2[0], e_lam_k2[0]])
    o_a = _attn_a(cfar[:A_HEADS], qa_t, ka, va_t, bias[:A_HEADS], lam_vecs,
                  e_subln[0][:, None], lam_init)
    o_b = _attn_b(cfar[A_HEADS:], qi_t, ki, wi_t, qb_t, kb, vb_t, bias[A_HEADS:], top_k)
    x2 = x.reshape(n, d)
    h1 = _out_proj([o_a.reshape(n, -1), o_b.reshape(n, -1)], gate0.reshape(n, -1), x2,
                   e_w_o[0].astype(BF16))

    w_in1 = o_w_in[0]
    w_ql = w_in1[:, :Q_LORA]
    w_kvl = w_in1[:, Q_LORA:Q_LORA + KV_LORA]
    w_kr = w_in1[:, Q_LORA + KV_LORA:Q_LORA + KV_LORA + C_ROPE]
    w_gate1 = w_in1[:, Q_LORA + KV_LORA + C_ROPE:].astype(BF16)
    kr_slot = _rope_slot(w_kr[:, None, :])[:, 0]
    kr_rot_slot = _rope_slot(_rot_cols(w_kr)[:, None, :])[:, 0]
    w_lat = jnp.concatenate([w_ql, w_kvl, kr_slot, kr_rot_slot], axis=1).astype(BF16)

    w_uq = o_w_uq[0].reshape(Q_LORA, C_HEADS, C_NOPE + C_ROPE)
    w_q = jnp.pad(w_uq[..., :C_NOPE], ((0, 0), (0, 0), (0, LANES - C_NOPE))) \
        + _rope_slot(w_uq[..., C_NOPE:])
    w_qrot = _rope_slot(_rot_cols(w_uq[..., C_NOPE:]))
    w_ukv = o_w_ukv[0].reshape(KV_LORA, C_HEADS, C_NOPE + C_V)
    w_k = jnp.pad(w_ukv[..., :C_NOPE], ((0, 0), (0, 0), (0, LANES - C_NOPE)))
    w_v = w_ukv[..., C_NOPE:]
    flat = lambda w: w.reshape(w.shape[0], -1)

    inv = ROPE_THETA ** (-jnp.arange(0, C_ROPE, 2, dtype=F32) / C_ROPE)
    ang = jnp.arange(s, dtype=F32)[:, None] * inv[None, :]
    cos, sin = jnp.cos(ang), jnp.sin(ang)
    pad_hi = jnp.zeros((s, LANES - C_NOPE - C_ROPE), F32)
    cos_tab = jnp.concatenate([jnp.ones((s, C_NOPE), F32), cos, cos, pad_hi], axis=1)
    sin_tab = jnp.concatenate([jnp.zeros((s, C_NOPE), F32), sin, sin, pad_hi], axis=1)

    q1_t, k1, v1_t, gate1 = _proj1(
        h1.reshape(b, s, d), o_norm[0][None], cos_tab, sin_tab, cos_tab.T, sin_tab.T,
        jnp.concatenate([w_lat, w_gate1], axis=1), w_gate1.shape[1],
        o_q_norm[0][None], o_kv_norm[0][None],
        _t_bf16(jnp.concatenate([flat(w_q), flat(w_qrot)], axis=1)),
        flat(w_k).astype(BF16), _t_bf16(flat(w_v)), ((C_NOPE + C_ROPE) ** -0.5) * LOG2E)
    o_c = _attn_c(q1_t, k1, v1_t)
    out = _out_proj([o_c.reshape(n, -1)], gate1.reshape(n, -1), h1, o_w_o[0].astype(BF16),
                    final_norm[None])
    return out.reshape(b, s, d)
```

```python
import functools
import math

import numpy as np
import jax
import jax.numpy as jnp
from jax import lax
from jax.experimental import pallas as pl
from jax.experimental.pallas import tpu as pltpu

F32 = jnp.float32
BF16 = jnp.bfloat16

CHUNK = 64
RMS_EPS = 1e-6
N_BUCKETS = 32
MAX_DISTANCE = 128
A_HEADS = 4
A_HEAD_DIM = 64
A_V_DIM = 2 * A_HEAD_DIM
B_HEADS = 8
B_HEAD_DIM = 64
IDX_HEADS = 8
IDX_DIM = 64
TOPK_MAX = 256
C_HEADS = 16
C_NOPE = 64
C_ROPE = 32
C_V = 64
Q_LORA = 384
KV_LORA = 256
ROPE_THETA = 10000.0

LANES = 128
BF16_ROWS = 16
ATT_TILE = 256
ROW_TILE = 512
VMEM_LIMIT = 56 * 1024 * 1024

C_GROUP = 8
COUNT_ROWS = 32
LOOKAHEAD = 8
LOG2E = math.log2(math.e)
NEG = -1e30
INT_MIN = -(2 ** 31)
NT_DIMS = (((1,), (1,)), ((), ()))
CHUNK_SHIFT = CHUNK.bit_length() - 1
assert 1 << CHUNK_SHIFT == CHUNK


def _t5_thresholds():
    nb = N_BUCKETS // 2
    max_exact = nb // 2
    n = np.arange(max_exact, MAX_DISTANCE + 1, dtype=np.float64)
    large = max_exact + (np.log(n / max_exact) / math.log(MAX_DISTANCE / max_exact)
                         * (nb - max_exact)).astype(np.int64)
    large = np.minimum(large, nb - 1)
    thr = [int(n[np.argmax(large >= b)]) for b in range(max_exact + 1, nb)]
    return max_exact, thr


T5_MAX_EXACT, T5_THRESHOLDS = _t5_thresholds()
T5_FAR_BUCKET = N_BUCKETS // 2 - 1


def _cparams(sem):
    return pltpu.CompilerParams(dimension_semantics=sem, vmem_limit_bytes=VMEM_LIMIT)


def _rms(x, g):
    return x * lax.rsqrt(jnp.mean(x * x, axis=-1, keepdims=True) + RMS_EPS) * g


def _silu(g):
    return g / (1.0 + jnp.exp(-g))


def _chunk_visible_t(shape):
    key = lax.broadcasted_iota(jnp.int32, shape, 0)
    qry = lax.broadcasted_iota(jnp.int32, shape, 1)
    return (key >> CHUNK_SHIFT) <= (qry >> CHUNK_SHIFT)


def _nt(w_t, x):
    return lax.dot_general(w_t, x, NT_DIMS, preferred_element_type=F32)


def _bias_tile_kernel(tab_ref, o_ref):
    h = pl.program_id(0)
    t = pl.program_id(1)
    shape = (ATT_TILE, ATT_TILE)
    key = lax.broadcasted_iota(jnp.int32, shape, 0)
    qry = lax.broadcasted_iota(jnp.int32, shape, 1)
    rel = key - qry - t * ATT_TILE
    n = jnp.abs(rel)
    large = jnp.full(shape, T5_MAX_EXACT, jnp.int32)
    for thr in T5_THRESHOLDS:
        large = large + jnp.where(n >= thr, 1, 0)
    bucket = jnp.where(rel > 0, N_BUCKETS // 2, 0) + jnp.where(n < T5_MAX_EXACT, n, large)
    val = jnp.full(shape, tab_ref[0, h], F32)
    for b in range(1, N_BUCKETS):
        val = jnp.where(bucket == b, tab_ref[b, h], val)
    visible = _chunk_visible_t(shape) | (t > 0)
    o_ref[0, 0] = jnp.where(visible, val * LOG2E, NEG)


def _bias_tiles(rel_bias):
    heads = rel_bias.shape[1]
    return pl.pallas_call(
        _bias_tile_kernel,
        grid=(heads, 2),
        in_specs=[pl.BlockSpec(memory_space=pltpu.SMEM)],
        out_specs=pl.BlockSpec((1, 1, ATT_TILE, ATT_TILE), lambda h, t: (h, t, 0, 0)),
        out_shape=jax.ShapeDtypeStruct((heads, 2, ATT_TILE, ATT_TILE), F32),
        compiler_params=_cparams(("arbitrary", "arbitrary")),
    )(rel_bias)


def _proj0_kernel(x_ref, g_ref, w_tok_ref, w_feat_ref, ka_ref, kb_ref, ki_ref, gate_ref,
                  qa_ref, va_ref, qb_ref, qi_ref, vb_ref, wi_ref, *, q_scale):
    hn = _rms(x_ref[0], g_ref[...]).astype(BF16)
    tok = jnp.dot(hn, w_tok_ref[...], preferred_element_type=F32)
    ka_ref[0] = tok[:, :512].astype(BF16)
    kb_ref[0] = tok[:, 512:512 + B_HEAD_DIM].astype(BF16)
    ki_ref[0] = tok[:, 512 + LANES:512 + LANES + IDX_DIM].astype(BF16)
    gate_ref[0] = tok[:, 512 + 2 * LANES:].astype(BF16)
    feat = _nt(w_feat_ref[...], hn)
    off = 0
    for o_ref, scale in ((qa_ref, q_scale), (va_ref, None), (qb_ref, q_scale), (qi_ref, None),
                         (vb_ref, None), (wi_ref, None)):
        rows = o_ref.shape[1]
        y = feat[off:off + rows]
        o_ref[0] = (y if scale is None else y * scale).astype(o_ref.dtype)
        off += rows


def _proj0(x, g, w_tok, w_feat_t, feat_rows, gate_cols, q_scale):
    b, s, d = x.shape
    tok = lambda c, dt: (pl.BlockSpec((1, ROW_TILE, c), lambda bi, i: (bi, i, 0)),
                         jax.ShapeDtypeStruct((b, s, c), dt))
    feat = lambda r, dt: (pl.BlockSpec((1, r, ROW_TILE), lambda bi, i: (bi, 0, i)),
                          jax.ShapeDtypeStruct((b, r, s), dt))
    full = lambda a: pl.BlockSpec(a.shape, lambda bi, i: (0, 0))
    assert sum(feat_rows) == w_feat_t.shape[0]
    outs = [tok(512, BF16), tok(B_HEAD_DIM, BF16), tok(IDX_DIM, BF16), tok(gate_cols, BF16)]
    outs += [feat(r, BF16) for r in feat_rows[:-1]] + [feat(feat_rows[-1], F32)]
    return pl.pallas_call(
        functools.partial(_proj0_kernel, q_scale=q_scale),
        grid=(b, s // ROW_TILE),
        in_specs=[pl.BlockSpec((1, ROW_TILE, d), lambda bi, i: (bi, i, 0)), full(g),
                  full(w_tok), full(w_feat_t)],
        out_specs=[o[0] for o in outs],
        out_shape=[o[1] for o in outs],
        compiler_params=_cparams(("arbitrary", "arbitrary")),
    )(x, g, w_tok, w_feat_t)


def _gated_residual(mixes, gate, x, w_ref):
    sg = _silu(gate.astype(F32))
    y = x
    off = 0
    for m in mixes:
        c = m.shape[1]
        a = (m.astype(F32) * sg[:, off:off + c]).astype(BF16)
        y = y + jnp.dot(a, w_ref[off:off + c, :], preferred_element_type=F32)
        off += c
    return y


def _mid_kernel(oa_ref, ob_ref, gate0_ref, x_ref, wo_ref, g_ref, cos_ref, sin_ref, cos_t_ref,
                sin_t_ref, w_tok_ref, qn_ref, kvn_ref, wqq_ref, wk_ref, wv_ref,
                h_ref, q_ref, k_ref, v_ref, gate_ref, *, q_scale):
    h = _gated_residual([oa_ref[0], ob_ref[0]], gate0_ref[0], x_ref[0], wo_ref)
    h_ref[0] = h
    hn = _rms(h, g_ref[...]).astype(BF16)
    tok = jnp.dot(hn, w_tok_ref[...], preferred_element_type=F32)
    q_lat = tok[:, :Q_LORA]
    kv_lat = tok[:, Q_LORA:Q_LORA + KV_LORA]
    kr = tok[:, Q_LORA + KV_LORA:Q_LORA + KV_LORA + LANES]
    kr_rot = tok[:, Q_LORA + KV_LORA + LANES:Q_LORA + KV_LORA + 2 * LANES]
    gate_ref[0] = tok[:, Q_LORA + KV_LORA + 2 * LANES:].astype(BF16)
    k_rope = kr * cos_ref[...] + kr_rot * sin_ref[...]
    qn = _rms(q_lat, qn_ref[...]).astype(BF16)
    kvn = _rms(kv_lat, kvn_ref[...]).astype(BF16)
    cos_t = cos_t_ref[...] * q_scale
    sin_t = sin_t_ref[...] * q_scale
    qq = _nt(wqq_ref[...], qn)
    kn = jnp.dot(kvn, wk_ref[...], preferred_element_type=F32)
    for hd in range(C_HEADS):
        sl = slice(hd * LANES, (hd + 1) * LANES)
        rot = slice((C_HEADS + hd) * LANES, (C_HEADS + hd + 1) * LANES)
        q_ref[0, sl, :] = (qq[sl] * cos_t + qq[rot] * sin_t).astype(BF16)
        k_ref[0, :, sl] = (kn[:, sl] + k_rope).astype(BF16)
    v_ref[0] = _nt(wv_ref[...], kvn).astype(BF16)


def _mid(o_a, o_b, gate0, x, w_o, g, cos, sin, cos_t, sin_t, w_tok, gate_cols, qn_g, kvn_g, wqq_t,
         w_k, wv_t, q_scale):
    b, s, d = x.shape
    tok_spec = lambda c: pl.BlockSpec((1, ROW_TILE, c), lambda bi, i: (bi, i, 0))
    tok = lambda c, dt: (tok_spec(c), jax.ShapeDtypeStruct((b, s, c), dt))
    feat = lambda r, dt: (pl.BlockSpec((1, r, ROW_TILE), lambda bi, i: (bi, 0, i)),
                          jax.ShapeDtypeStruct((b, r, s), dt))
    full = lambda a: pl.BlockSpec(a.shape, lambda bi, i: (0, 0))
    outs = [tok(d, F32), feat(C_HEADS * LANES, BF16), tok(C_HEADS * LANES, BF16),
            feat(C_HEADS * C_V, BF16), tok(gate_cols, BF16)]
    return pl.pallas_call(
        functools.partial(_mid_kernel, q_scale=q_scale),
        grid=(b, s // ROW_TILE),
        in_specs=[tok_spec(o_a.shape[2]), tok_spec(o_b.shape[2]), tok_spec(gate0.shape[2]),
                  tok_spec(d), full(w_o), full(g),
                  pl.BlockSpec((ROW_TILE, LANES), lambda bi, i: (i, 0)),
                  pl.BlockSpec((ROW_TILE, LANES), lambda bi, i: (i, 0)),
                  pl.BlockSpec((LANES, ROW_TILE), lambda bi, i: (0, i)),
                  pl.BlockSpec((LANES, ROW_TILE), lambda bi, i: (0, i)),
                  full(w_tok), full(qn_g), full(kvn_g), full(wqq_t), full(w_k), full(wv_t)],
        out_specs=[o[0] for o in outs],
        out_shape=[o[1] for o in outs],
        compiler_params=_cparams(("arbitrary", "arbitrary")),
    )(o_a, o_b, gate0, x, w_o, g, cos, sin, cos_t, sin_t, w_tok, qn_g, kvn_g, wqq_t, w_k, wv_t)


def _final_kernel(o_ref, gate_ref, h_ref, w_ref, g_ref, out_ref):
    out_ref[0] = _rms(_gated_residual([o_ref[0]], gate_ref[0], h_ref[0], w_ref), g_ref[...])


def _final(o_c, gate, h, w_o, final_g):
    b, s, d = h.shape
    tok_spec = lambda c: pl.BlockSpec((1, ROW_TILE, c), lambda bi, i: (bi, i, 0))
    full = lambda a: pl.BlockSpec(a.shape, lambda bi, i: (0, 0))
    return pl.pallas_call(
        _final_kernel,
        grid=(b, s // ROW_TILE),
        in_specs=[tok_spec(o_c.shape[2]), tok_spec(gate.shape[2]), tok_spec(d), full(w_o),
                  full(final_g)],
        out_specs=tok_spec(d),
        out_shape=jax.ShapeDtypeStruct((b, s, d), F32),
        compiler_params=_cparams(("arbitrary", "arbitrary")),
    )(o_c, gate, h, w_o, final_g)


def _with_ones(v_t):
    return jnp.concatenate([v_t, jnp.ones((BF16_ROWS, v_t.shape[1]), v_t.dtype)], axis=0)


def _softmax_pv(s2, v_aug, m_ref, acc_ref, shift, split):
    m_old = m_ref[...]
    if split:
        half = s2.shape[0] // 2
        s_a, s_b = s2[:half], s2[half:]
        m_a = jnp.maximum(m_old, jnp.max(s_a, axis=0, keepdims=True) + shift)
        p_a = jnp.exp2(s_a - (m_a - shift)).astype(v_aug.dtype)
        m_new = jnp.maximum(m_a, jnp.max(s_b, axis=0, keepdims=True) + shift)
        p_b = jnp.exp2(s_b - (m_new - shift)).astype(v_aug.dtype)
        p = jnp.concatenate([p_a * jnp.exp2(m_a - m_new).astype(v_aug.dtype), p_b], axis=0)
    else:
        m_new = jnp.maximum(m_old, jnp.max(s2, axis=0, keepdims=True) + shift)
        p = jnp.exp2(s2 - (m_new - shift)).astype(v_aug.dtype)
    alpha = jnp.exp2(m_old - m_new)
    acc_ref[...] = alpha * acc_ref[...] + jnp.dot(v_aug, p, preferred_element_type=F32)
    m_ref[...] = m_new


def _init_flash(m_ref, acc_ref):
    m_ref[...] = jnp.full(m_ref.shape, -jnp.inf, F32)
    acc_ref[...] = jnp.zeros(acc_ref.shape, F32)


def _normalised(acc, dv):
    return acc[:dv] / acc[dv:dv + 1]


def _staggered(jobs, lookahead):
    lookahead = len(jobs) if lookahead is None else lookahead
    scores = [None] * len(jobs)
    for i in range(min(lookahead, len(jobs))):
        scores[i] = jobs[i][0]()
    for i, (_, consume) in enumerate(jobs):
        if i + lookahead < len(jobs):
            scores[i + lookahead] = jobs[i + lookahead][0]()
        consume(scores[i])
        scores[i] = None


def _key_tile_loop(qt, jobs_fn, lookahead, has_near):
    n_plain = jnp.maximum(qt - 1, 0) if has_near else qt

    def pair_body(i, carry):
        _staggered(jobs_fn(2 * i, None) + jobs_fn(2 * i + 1, None), lookahead)
        return carry
    lax.fori_loop(0, n_plain >> 1, pair_body, 0)
    odd = (n_plain & 1) == 1
    if has_near:
        @pl.when(odd)
        def _():
            _staggered(jobs_fn(n_plain - 1, None), lookahead)

        @pl.when(qt >= 1)
        def _():
            _staggered(jobs_fn(qt - 1, 1) + jobs_fn(qt, 0), lookahead)

        @pl.when(qt == 0)
        def _():
            _staggered(jobs_fn(qt, 0), lookahead)
    else:
        @pl.when(odd)
        def _():
            _staggered(jobs_fn(qt - 1, None) + jobs_fn(qt, 0), lookahead)

        @pl.when(jnp.logical_not(odd))
        def _():
            _staggered(jobs_fn(qt, 0), lookahead)


def _attn_a_kernel(cfar_ref, q_ref, k_ref, v_ref, bias_ref, lam_ref, g_ref, o_ref,
                   m_ref, acc_ref, *, lam_init):
    qt = pl.program_id(1)
    t = ATT_TILE
    _init_flash(m_ref, acc_ref)

    def jobs(j, bias_idx):
        k0 = pl.multiple_of(j * t, t)

        def make(c):
            h = c // 2

            def qk():
                return jnp.dot(k_ref[0, pl.ds(k0, t), h * LANES:(h + 1) * LANES],
                               q_ref[0, c * LANES:(c + 1) * LANES, :],
                               preferred_element_type=F32)

            def consume(score):
                v_aug = _with_ones(v_ref[0, h * A_V_DIM:(h + 1) * A_V_DIM, pl.ds(k0, t)])
                if bias_idx is None:
                    s2, shift = score, cfar_ref[h]
                else:
                    s2, shift = bias_ref[h, bias_idx] + score, 0.0
                _softmax_pv(s2, v_aug, m_ref.at[c], acc_ref.at[c], shift, split=True)
            return qk, consume
        return [make(c) for c in range(2 * A_HEADS)]

    _key_tile_loop(qt, jobs, None, has_near=True)

    lam_v = lam_ref[...]
    lam = (jnp.exp(jnp.sum(lam_v[0:1] * lam_v[1:2], axis=-1, keepdims=True))
           - jnp.exp(jnp.sum(lam_v[2:3] * lam_v[3:4], axis=-1, keepdims=True)) + lam_init)
    outs = []
    for h in range(A_HEADS):
        o = (_normalised(acc_ref[2 * h], A_V_DIM)
             - lam * _normalised(acc_ref[2 * h + 1], A_V_DIM))
        inv = lax.rsqrt(jnp.mean(o * o, axis=0, keepdims=True) + RMS_EPS)
        outs.append(o * inv * (g_ref[...] * (1.0 - lam_init)))
    o_ref[0] = jnp.concatenate(outs, axis=0).T.astype(o_ref.dtype)


def _attn_a(cfar, qa_t, ka, va_t, bias, lam_vecs, subln_col, lam_init):
    b, s, _ = ka.shape
    nq = s // ATT_TILE
    chains = 2 * A_HEADS
    return pl.pallas_call(
        functools.partial(_attn_a_kernel, lam_init=lam_init),
        grid=(b, nq),
        in_specs=[pl.BlockSpec(memory_space=pltpu.SMEM),
                  pl.BlockSpec((1, chains * LANES, ATT_TILE), lambda bi, qi: (bi, 0, qi)),
                  pl.BlockSpec((1, s, ka.shape[2]), lambda bi, qi: (bi, 0, 0)),
                  pl.BlockSpec((1, va_t.shape[1], s), lambda bi, qi: (bi, 0, 0)),
                  pl.BlockSpec(bias.shape, lambda bi, qi: (0, 0, 0, 0)),
                  pl.BlockSpec(lam_vecs.shape, lambda bi, qi: (0, 0)),
                  pl.BlockSpec(subln_col.shape, lambda bi, qi: (0, 0))],
        out_specs=pl.BlockSpec((1, ATT_TILE, A_HEADS * A_V_DIM), lambda bi, qi: (bi, qi, 0)),
        out_shape=jax.ShapeDtypeStruct((b, s, A_HEADS * A_V_DIM), BF16),
        scratch_shapes=[pltpu.VMEM((chains, 1, ATT_TILE), F32),
                        pltpu.VMEM((chains, A_V_DIM + BF16_ROWS, ATT_TILE), F32)],
        compiler_params=_cparams(("arbitrary", "arbitrary")),
    )(cfar, qa_t, ka, va_t, bias, lam_vecs, subln_col)


def _attn_b_kernel(cfar_ref, qi_ref, ki_ref, wi_ref, qb_ref, kb_ref, vb_ref, bias_ref, o_ref,
                   keys_ref, seen_ref, m_ref, acc_ref, *, top_k):
    qt = pl.program_id(1)
    n_tiles = qt + 1
    t = ATT_TILE
    chunk_ok = _chunk_visible_t((t, t))

    def tile_start(j):
        return pl.multiple_of(j * t, t)

    w = wi_ref[0]

    def score_tile(j, carry):
        k0 = tile_start(j)
        kt = ki_ref[0, pl.ds(k0, t), :]
        logits = [jnp.dot(kt, qi_ref[0, hh * IDX_DIM:(hh + 1) * IDX_DIM, :],
                          preferred_element_type=F32) for hh in range(IDX_HEADS)]
        score = None
        for hh in range(IDX_HEADS):
            term = w[hh:hh + 1, :] * jnp.maximum(logits[hh], 0.0)
            score = term if score is None else score + term
        score = jnp.where(chunk_ok | (j < qt), score, -jnp.inf)
        bits = lax.bitcast_convert_type(score, jnp.int32)
        keys_ref[pl.ds(k0, t), :] = jnp.where(bits < 0, (bits ^ 0x7FFFFFFF) + 1, bits)
        return carry

    def score_pair(i, carry):
        score_tile(2 * i, carry)
        return score_tile(2 * i + 1, carry)

    lax.fori_loop(0, n_tiles >> 1, score_pair, 0)

    @pl.when((n_tiles & 1) == 1)
    def _():
        score_tile(n_tiles - 1, 0)

    def count(pred_fn):
        def body(j, acc):
            hit = pred_fn(keys_ref[pl.ds(tile_start(j), t), :])
            return acc + jnp.sum(hit.reshape(t // COUNT_ROWS, COUNT_ROWS, t), axis=0)
        acc = lax.fori_loop(0, n_tiles, body, jnp.zeros((COUNT_ROWS, t), F32))
        return jnp.sum(acc, axis=0, keepdims=True)

    kf = float(top_k)

    def value_step(i, thr):
        cand = thr + lax.shift_left(jnp.int32(1), jnp.asarray(31 - i, jnp.int32))
        cnt = count(lambda blk: jnp.where(blk >= cand, 1.0, 0.0))
        return jnp.where(cnt >= kf, cand, thr)

    thr = lax.fori_loop(0, 32, value_step, jnp.full((1, t), INT_MIN, jnp.int32))
    need = kf - count(lambda blk: jnp.where(blk > thr, 1.0, 0.0))
    tri = jnp.where(lax.broadcasted_iota(jnp.int32, (t, t), 0)
                    >= lax.broadcasted_iota(jnp.int32, (t, t), 1), 1.0, 0.0).astype(BF16)

    _init_flash(m_ref, acc_ref)
    seen_ref[...] = jnp.zeros(seen_ref.shape, F32)

    def jobs(j, bias_idx):
        k0 = tile_start(j)
        tile = {}

        def rank_matmul():
            eq = jnp.where(keys_ref[pl.ds(k0, t), :] == thr, 1.0, 0.0)
            return jnp.dot(tri, eq.astype(BF16), preferred_element_type=F32)

        def make_mask(in_tile_rank):
            blk = keys_ref[pl.ds(k0, t), :]
            rank = seen_ref[...] + in_tile_rank
            seen_ref[...] = rank[t - 1:t, :]
            tie_keep = jnp.where(rank <= need, 0.0, NEG)
            tile["mask"] = jnp.where(blk > thr, 0.0, jnp.where(blk == thr, tie_keep, NEG))

        def make(hh):
            def qk():
                return jnp.dot(kb_ref[0, pl.ds(k0, t), :],
                               qb_ref[0, hh * B_HEAD_DIM:(hh + 1) * B_HEAD_DIM, :],
                               preferred_element_type=F32)

            def consume(score):
                v_aug = _with_ones(vb_ref[0, :, pl.ds(k0, t)])
                if bias_idx is None:
                    s2, shift = tile["mask"] + score, cfar_ref[hh]
                else:
                    s2, shift = (tile["mask"] + bias_ref[hh, bias_idx]) + score, 0.0
                _softmax_pv(s2, v_aug, m_ref.at[hh], acc_ref.at[hh], shift, split=False)
            return qk, consume
        return [(rank_matmul, make_mask)] + [make(hh) for hh in range(B_HEADS)]

    _key_tile_loop(qt, jobs, None, has_near=True)

    o_t = jnp.concatenate([_normalised(acc_ref[hh], B_HEAD_DIM) for hh in range(B_HEADS)], axis=0)
    o_ref[0] = o_t.T.astype(o_ref.dtype)


def _attn_b(cfar, qi_t, ki, wi_t, qb_t, kb, vb_t, bias, top_k):
    b, s, _ = kb.shape
    nq = s // ATT_TILE
    qmap = lambda bi, qt: (bi, 0, qt)
    kmap = lambda bi, qt: (bi, 0, 0)
    return pl.pallas_call(
        functools.partial(_attn_b_kernel, top_k=top_k),
        grid=(b, nq),
        in_specs=[pl.BlockSpec(memory_space=pltpu.SMEM),
                  pl.BlockSpec((1, qi_t.shape[1], ATT_TILE), qmap),
                  pl.BlockSpec((1, s, ki.shape[2]), kmap),
                  pl.BlockSpec((1, wi_t.shape[1], ATT_TILE), qmap),
                  pl.BlockSpec((1, qb_t.shape[1], ATT_TILE), qmap),
                  pl.BlockSpec((1, s, kb.shape[2]), kmap),
                  pl.BlockSpec((1, vb_t.shape[1], s), kmap),
                  pl.BlockSpec(bias.shape, lambda bi, qt: (0, 0, 0, 0))],
        out_specs=pl.BlockSpec((1, ATT_TILE, B_HEADS * B_HEAD_DIM), lambda bi, qt: (bi, qt, 0)),
        out_shape=jax.ShapeDtypeStruct((b, s, B_HEADS * B_HEAD_DIM), BF16),
        scratch_shapes=[pltpu.VMEM((s, ATT_TILE), jnp.int32),
                        pltpu.VMEM((1, ATT_TILE), F32),
                        pltpu.VMEM((B_HEADS, 1, ATT_TILE), F32),
                        pltpu.VMEM((B_HEADS, B_HEAD_DIM + BF16_ROWS, ATT_TILE), F32)],
        compiler_params=_cparams(("arbitrary", "arbitrary")),
    )(cfar, qi_t, ki, wi_t, qb_t, kb, vb_t, bias)


def _attn_c_kernel(q_ref, k_ref, v_ref, o_ref, m_ref, acc_ref):
    qt = pl.program_id(2)
    t = ATT_TILE
    chunk_ok = _chunk_visible_t((t, t))
    _init_flash(m_ref, acc_ref)

    def jobs(j, kind):
        k0 = pl.multiple_of(j * t, t)
        diagonal = kind == 0

        def make(g):
            def qk():
                return jnp.dot(k_ref[0, pl.ds(k0, t), g * LANES:(g + 1) * LANES],
                               q_ref[0, g * LANES:(g + 1) * LANES, :],
                               preferred_element_type=F32)

            def consume(score):
                s2 = jnp.where(chunk_ok, score, NEG) if diagonal else score
                v_aug = _with_ones(v_ref[0, g * C_V:(g + 1) * C_V, pl.ds(k0, t)])
                _softmax_pv(s2, v_aug, m_ref.at[g], acc_ref.at[g], 0.0, split=True)
            return qk, consume
        return [make(g) for g in range(C_GROUP)]

    _key_tile_loop(qt, jobs, LOOKAHEAD, has_near=False)
    o_t = jnp.concatenate([_normalised(acc_ref[g], C_V) for g in range(C_GROUP)], axis=0)
    o_ref[0] = o_t.T.astype(o_ref.dtype)


def _attn_c(q_t, k, v_t):
    b, s, _ = k.shape
    nq = s // ATT_TILE
    groups = C_HEADS // C_GROUP
    return pl.pallas_call(
        _attn_c_kernel,
        grid=(b, groups, nq),
        in_specs=[pl.BlockSpec((1, C_GROUP * LANES, ATT_TILE), lambda bi, p, qi: (bi, p, qi)),
                  pl.BlockSpec((1, s, C_GROUP * LANES), lambda bi, p, qi: (bi, 0, p)),
                  pl.BlockSpec((1, C_GROUP * C_V, s), lambda bi, p, qi: (bi, p, 0))],
        out_specs=pl.BlockSpec((1, ATT_TILE, C_GROUP * C_V), lambda bi, p, qi: (bi, qi, p)),
        out_shape=jax.ShapeDtypeStruct((b, s, C_HEADS * C_V), BF16),
        scratch_shapes=[pltpu.VMEM((C_GROUP, 1, ATT_TILE), F32),
                        pltpu.VMEM((C_GROUP, C_V + BF16_ROWS, ATT_TILE), F32)],
        compiler_params=_cparams(("arbitrary", "arbitrary", "arbitrary")),
    )(q_t, k, v_t)


def _rot_cols(w):
    half = w.shape[-1] // 2
    return jnp.concatenate([-w[..., half:], w[..., :half]], axis=-1)


def _rope_slot(w_rope):
    return jnp.pad(w_rope, ((0, 0), (0, 0), (C_NOPE, LANES - C_NOPE - C_ROPE)))


def _t_bf16(w):
    return w.T.astype(BF16)


def kernel(x, rel_bias, e_norm, e_w_in, e_lam_q1, e_lam_k1, e_lam_q2, e_lam_k2, e_subln, e_w_o,
           o_norm, o_w_in, o_q_norm, o_w_uq, o_kv_norm, o_w_ukv, o_w_o, final_norm):
    b, s, d = x.shape
    assert s % ROW_TILE == 0 and s % ATT_TILE == 0 and ATT_TILE % CHUNK == 0
    assert ATT_TILE >= MAX_DISTANCE, "key tiles two or more before the diagonal must be all-far"
    top_k = min(TOPK_MAX, s // 4)
    assert top_k <= ATT_TILE, "every query must see at least top_k keys inside its own window"

    bias = _bias_tiles(rel_bias)
    cfar = rel_bias[T5_FAR_BUCKET] * LOG2E

    w_in = e_w_in[0]
    sizes = (512, 512, 512, 512, 64, 64, 512, 64, 8, 1024)
    offs = np.concatenate([[0], np.cumsum(sizes)])
    col = lambda i: w_in[:, offs[i]:offs[i + 1]]
    zeros64 = jnp.zeros((d, LANES - B_HEAD_DIM), F32)
    w_tok = jnp.concatenate([col(1), col(4), zeros64, col(7), zeros64, col(9)],
                            axis=1).astype(BF16)
    wqa = col(0).reshape(d, A_HEADS, 2, A_HEAD_DIM)
    zq = jnp.zeros((d, A_HEADS, A_HEAD_DIM), F32)
    wqa = jnp.stack([jnp.concatenate([wqa[:, :, 0], zq], axis=-1),
                     jnp.concatenate([zq, wqa[:, :, 1]], axis=-1)], axis=2)
    feat_w = [wqa.reshape(d, 2 * A_HEADS * LANES), col(2), col(3), col(6), col(5),
              jnp.pad(col(8), ((0, 0), (0, BF16_ROWS - IDX_HEADS)))]
    ka, kb, ki, gate0, qa_t, va_t, qb_t, qi_t, vb_t, wi_t = _proj0(
        x, e_norm[0][None], w_tok, _t_bf16(jnp.concatenate(feat_w, axis=1)),
        [w.shape[1] for w in feat_w], sizes[9], (A_HEAD_DIM ** -0.5) * LOG2E)

    lam_init = 0.8 - 0.6 * math.exp(-0.3 * 0)
    lam_vecs = jnp.stack([e_lam_q1[0], e_lam_k1[0], e_lam_q2[0], e_lam_k2[0]])
    o_a = _attn_a(cfar[:A_HEADS], qa_t, ka, va_t, bias[:A_HEADS], lam_vecs,
                  e_subln[0][:, None], lam_init)
    o_b = _attn_b(cfar[A_HEADS:], qi_t, ki, wi_t, qb_t, kb, vb_t, bias[A_HEADS:], top_k)

    w_in1 = o_w_in[0]
    w_ql = w_in1[:, :Q_LORA]
    w_kvl = w_in1[:, Q_LORA:Q_LORA + KV_LORA]
    w_kr = w_in1[:, Q_LORA + KV_LORA:Q_LORA + KV_LORA + C_ROPE]
    w_gate1 = w_in1[:, Q_LORA + KV_LORA + C_ROPE:].astype(BF16)
    kr_slot = _rope_slot(w_kr[:, None, :])[:, 0]
    kr_rot_slot = _rope_slot(_rot_cols(w_kr)[:, None, :])[:, 0]
    w_lat = jnp.concatenate([w_ql, w_kvl, kr_slot, kr_rot_slot], axis=1).astype(BF16)

    w_uq = o_w_uq[0].reshape(Q_LORA, C_HEADS, C_NOPE + C_ROPE)
    w_q = jnp.pad(w_uq[..., :C_NOPE], ((0, 0), (0, 0), (0, LANES - C_NOPE))) \
        + _rope_slot(w_uq[..., C_NOPE:])
    w_qrot = _rope_slot(_rot_cols(w_uq[..., C_NOPE:]))
    w_ukv = o_w_ukv[0].reshape(KV_LORA, C_HEADS, C_NOPE + C_V)
    w_k = jnp.pad(w_ukv[..., :C_NOPE], ((0, 0), (0, 0), (0, LANES - C_NOPE)))
    w_v = w_ukv[..., C_NOPE:]
    flat = lambda w: w.reshape(w.shape[0], -1)

    inv = ROPE_THETA ** (-jnp.arange(0, C_ROPE, 2, dtype=F32) / C_ROPE)
    ang = jnp.arange(s, dtype=F32)[:, None] * inv[None, :]
    cos, sin = jnp.cos(ang), jnp.sin(ang)
    pad_hi = jnp.zeros((s, LANES - C_NOPE - C_ROPE), F32)
    cos_tab = jnp.concatenate([jnp.ones((s, C_NOPE), F32), cos, cos, pad_hi], axis=1)
    sin_tab = jnp.concatenate([jnp.zeros((s, C_NOPE), F32), sin, sin, pad_hi], axis=1)

    h1, q1_t, k1, v1_t, gate1 = _mid(
        o_a, o_b, gate0, x, e_w_o[0].astype(BF16), o_norm[0][None], cos_tab, sin_tab, cos_tab.T,
        sin_tab.T, jnp.concatenate([w_lat, w_gate1], axis=1), w_gate1.shape[1],
        o_q_norm[0][None], o_kv_norm[0][None],
        _t_bf16(jnp.concatenate([flat(w_q), flat(w_qrot)], axis=1)),
        flat(w_k).astype(BF16), _t_bf16(flat(w_v)), ((C_NOPE + C_ROPE) ** -0.5) * LOG2E)
    o_c = _attn_c(q1_t, k1, v1_t)
    return _final(o_c, gate1, h1, o_w_o[0].astype(BF16), final_norm[None])
```

```python
import functools
import math

import numpy as np
import jax
import jax.numpy as jnp
from jax import lax
from jax.experimental import pallas as pl
from jax.experimental.pallas import tpu as pltpu

F32 = jnp.float32
BF16 = jnp.bfloat16

CHUNK = 64
RMS_EPS = 1e-6
N_BUCKETS = 32
MAX_DISTANCE = 128
A_HEADS = 4
A_HEAD_DIM = 64
A_V_DIM = 2 * A_HEAD_DIM
B_HEADS = 8
B_HEAD_DIM = 64
IDX_HEADS = 8
IDX_DIM = 64
TOPK_MAX = 256
C_HEADS = 16
C_NOPE = 64
C_ROPE = 32
C_V = 64
Q_LORA = 384
KV_LORA = 256
ROPE_THETA = 10000.0

LANES = 128
BF16_ROWS = 16
ATT_TILE = 256
ROW_TILE = 512
VMEM_LIMIT = 56 * 1024 * 1024

C_GROUP = 8
COUNT_ROWS = 32
EARLY_BITS = 25
LOOKAHEAD = 8
LOG2E = math.log2(math.e)
NEG = -1e30
INT_MIN = -(2 ** 31)
NT_DIMS = (((1,), (1,)), ((), ()))
CHUNK_SHIFT = CHUNK.bit_length() - 1
assert 1 << CHUNK_SHIFT == CHUNK


def _t5_thresholds():
    nb = N_BUCKETS // 2
    max_exact = nb // 2
    n = np.arange(max_exact, MAX_DISTANCE + 1, dtype=np.float64)
    large = max_exact + (np.log(n / max_exact) / math.log(MAX_DISTANCE / max_exact)
                         * (nb - max_exact)).astype(np.int64)
    large = np.minimum(large, nb - 1)
    thr = [int(n[np.argmax(large >= b)]) for b in range(max_exact + 1, nb)]
    return max_exact, thr


T5_MAX_EXACT, T5_THRESHOLDS = _t5_thresholds()
T5_FAR_BUCKET = N_BUCKETS // 2 - 1


def _cparams(sem):
    return pltpu.CompilerParams(dimension_semantics=sem, vmem_limit_bytes=VMEM_LIMIT)


def _rms(x, g):
    return x * lax.rsqrt(jnp.mean(x * x, axis=-1, keepdims=True) + RMS_EPS) * g


def _silu(g):
    return g / (1.0 + jnp.exp(-g))


def _chunk_visible_t(shape):
    key = lax.broadcasted_iota(jnp.int32, shape, 0)
    qry = lax.broadcasted_iota(jnp.int32, shape, 1)
    return (key >> CHUNK_SHIFT) <= (qry >> CHUNK_SHIFT)


def _nt(w_t, x):
    return lax.dot_general(w_t, x, NT_DIMS, preferred_element_type=F32)


def _bias_tile_kernel(tab_ref, o_ref):
    h = pl.program_id(0)
    t = pl.program_id(1)
    shape = (ATT_TILE, ATT_TILE)
    key = lax.broadcasted_iota(jnp.int32, shape, 0)
    qry = lax.broadcasted_iota(jnp.int32, shape, 1)
    rel = key - qry - t * ATT_TILE
    n = jnp.abs(rel)
    large = jnp.full(shape, T5_MAX_EXACT, jnp.int32)
    for thr in T5_THRESHOLDS:
        large = large + jnp.where(n >= thr, 1, 0)
    bucket = jnp.where(rel > 0, N_BUCKETS // 2, 0) + jnp.where(n < T5_MAX_EXACT, n, large)
    val = jnp.full(shape, tab_ref[0, h], F32)
    for b in range(1, N_BUCKETS):
        val = jnp.where(bucket == b, tab_ref[b, h], val)
    visible = _chunk_visible_t(shape) | (t > 0)
    o_ref[0, 0] = jnp.where(visible, val * LOG2E, NEG)


def _bias_tiles(rel_bias):
    heads = rel_bias.shape[1]
    return pl.pallas_call(
        _bias_tile_kernel,
        grid=(heads, 2),
        in_specs=[pl.BlockSpec(memory_space=pltpu.SMEM)],
        out_specs=pl.BlockSpec((1, 1, ATT_TILE, ATT_TILE), lambda h, t: (h, t, 0, 0)),
        out_shape=jax.ShapeDtypeStruct((heads, 2, ATT_TILE, ATT_TILE), F32),
        compiler_params=_cparams(("arbitrary", "arbitrary")),
    )(rel_bias)


def _proj0_kernel(x_ref, g_ref, w_tok_ref, w_feat_ref, ka_ref, kb_ref, ki_ref, gate_ref,
                  qa_ref, va_ref, qb_ref, qi_ref, vb_ref, wi_ref, *, q_scale):
    hn = _rms(x_ref[0], g_ref[...]).astype(BF16)
    tok = jnp.dot(hn, w_tok_ref[...], preferred_element_type=F32)
    ka_ref[0] = tok[:, :512].astype(BF16)
    kb_ref[0] = tok[:, 512:512 + B_HEAD_DIM].astype(BF16)
    ki_ref[0] = tok[:, 512 + LANES:512 + LANES + IDX_DIM].astype(BF16)
    gate_ref[0] = tok[:, 512 + 2 * LANES:].astype(BF16)
    feat = _nt(w_feat_ref[...], hn)
    off = 0
    for o_ref, scale in ((qa_ref, q_scale), (va_ref, None), (qb_ref, q_scale), (qi_ref, None),
                         (vb_ref, None), (wi_ref, None)):
        rows = o_ref.shape[1]
        y = feat[off:off + rows]
        o_ref[0] = (y if scale is None else y * scale).astype(o_ref.dtype)
        off += rows


def _proj0(x, g, w_tok, w_feat_t, feat_rows, gate_cols, q_scale):
    b, s, d = x.shape
    tok = lambda c, dt: (pl.BlockSpec((1, ROW_TILE, c), lambda bi, i: (bi, i, 0)),
                         jax.ShapeDtypeStruct((b, s, c), dt))
    feat = lambda r, dt: (pl.BlockSpec((1, r, ROW_TILE), lambda bi, i: (bi, 0, i)),
                          jax.ShapeDtypeStruct((b, r, s), dt))
    full = lambda a: pl.BlockSpec(a.shape, lambda bi, i: (0, 0))
    assert sum(feat_rows) == w_feat_t.shape[0]
    outs = [tok(512, BF16), tok(B_HEAD_DIM, BF16), tok(IDX_DIM, BF16), tok(gate_cols, BF16)]
    outs += [feat(r, BF16) for r in feat_rows[:-1]] + [feat(feat_rows[-1], F32)]
    return pl.pallas_call(
        functools.partial(_proj0_kernel, q_scale=q_scale),
        grid=(b, s // ROW_TILE),
        in_specs=[pl.BlockSpec((1, ROW_TILE, d), lambda bi, i: (bi, i, 0)), full(g),
                  full(w_tok), full(w_feat_t)],
        out_specs=[o[0] for o in outs],
        out_shape=[o[1] for o in outs],
        compiler_params=_cparams(("arbitrary", "arbitrary")),
    )(x, g, w_tok, w_feat_t)


def _gated_residual(mixes, gate, x, w_ref):
    sg = _silu(gate.astype(F32))
    y = x
    off = 0
    for m in mixes:
        c = m.shape[1]
        a = (m.astype(F32) * sg[:, off:off + c]).astype(BF16)
        y = y + jnp.dot(a, w_ref[off:off + c, :], preferred_element_type=F32)
        off += c
    return y


def _mid_kernel(oa_ref, ob_ref, gate0_ref, x_ref, wo_ref, g_ref, cos_ref, sin_ref, cos_t_ref,
                sin_t_ref, w_tok_ref, qn_ref, kvn_ref, wqq_ref, wk_ref, wv_ref,
                h_ref, q_ref, k_ref, v_ref, gate_ref, *, q_scale):
    h = _gated_residual([oa_ref[0], ob_ref[0]], gate0_ref[0], x_ref[0], wo_ref)
    h_ref[0] = h
    hn = _rms(h, g_ref[...]).astype(BF16)
    tok = jnp.dot(hn, w_tok_ref[...], preferred_element_type=F32)
    q_lat = tok[:, :Q_LORA]
    kv_lat = tok[:, Q_LORA:Q_LORA + KV_LORA]
    kr = tok[:, Q_LORA + KV_LORA:Q_LORA + KV_LORA + LANES]
    kr_rot = tok[:, Q_LORA + KV_LORA + LANES:Q_LORA + KV_LORA + 2 * LANES]
    gate_ref[0] = tok[:, Q_LORA + KV_LORA + 2 * LANES:].astype(BF16)
    k_rope = kr * cos_ref[...] + kr_rot * sin_ref[...]
    qn = _rms(q_lat, qn_ref[...]).astype(BF16)
    kvn = _rms(kv_lat, kvn_ref[...]).astype(BF16)
    cos_t = cos_t_ref[...] * q_scale
    sin_t = sin_t_ref[...] * q_scale
    qq = _nt(wqq_ref[...], qn)
    kn = jnp.dot(kvn, wk_ref[...], preferred_element_type=F32)
    for hd in range(C_HEADS):
        sl = slice(hd * LANES, (hd + 1) * LANES)
        rot = slice((C_HEADS + hd) * LANES, (C_HEADS + hd + 1) * LANES)
        q_ref[0, sl, :] = (qq[sl] * cos_t + qq[rot] * sin_t).astype(BF16)
        k_ref[0, :, sl] = (kn[:, sl] + k_rope).astype(BF16)
    v_ref[0] = _nt(wv_ref[...], kvn).astype(BF16)


def _mid(o_a, o_b, gate0, x, w_o, g, cos, sin, cos_t, sin_t, w_tok, gate_cols, qn_g, kvn_g, wqq_t,
         w_k, wv_t, q_scale):
    b, s, d = x.shape
    tok_spec = lambda c: pl.BlockSpec((1, ROW_TILE, c), lambda bi, i: (bi, i, 0))
    tok = lambda c, dt: (tok_spec(c), jax.ShapeDtypeStruct((b, s, c), dt))
    feat = lambda r, dt: (pl.BlockSpec((1, r, ROW_TILE), lambda bi, i: (bi, 0, i)),
                          jax.ShapeDtypeStruct((b, r, s), dt))
    full = lambda a: pl.BlockSpec(a.shape, lambda bi, i: (0, 0))
    outs = [tok(d, F32), feat(C_HEADS * LANES, BF16), tok(C_HEADS * LANES, BF16),
            feat(C_HEADS * C_V, BF16), tok(gate_cols, BF16)]
    return pl.pallas_call(
        functools.partial(_mid_kernel, q_scale=q_scale),
        grid=(b, s // ROW_TILE),
        in_specs=[tok_spec(o_a.shape[2]), tok_spec(o_b.shape[2]), tok_spec(gate0.shape[2]),
                  tok_spec(d), full(w_o), full(g),
                  pl.BlockSpec((ROW_TILE, LANES), lambda bi, i: (i, 0)),
                  pl.BlockSpec((ROW_TILE, LANES), lambda bi, i: (i, 0)),
                  pl.BlockSpec((LANES, ROW_TILE), lambda bi, i: (0, i)),
                  pl.BlockSpec((LANES, ROW_TILE), lambda bi, i: (0, i)),
                  full(w_tok), full(qn_g), full(kvn_g), full(wqq_t), full(w_k), full(wv_t)],
        out_specs=[o[0] for o in outs],
        out_shape=[o[1] for o in outs],
        compiler_params=_cparams(("arbitrary", "arbitrary")),
    )(o_a, o_b, gate0, x, w_o, g, cos, sin, cos_t, sin_t, w_tok, qn_g, kvn_g, wqq_t, w_k, wv_t)


def _final_kernel(o_ref, gate_ref, h_ref, w_ref, g_ref, out_ref):
    out_ref[0] = _rms(_gated_residual([o_ref[0]], gate_ref[0], h_ref[0], w_ref), g_ref[...])


def _final(o_c, gate, h, w_o, final_g):
    b, s, d = h.shape
    tok_spec = lambda c: pl.BlockSpec((1, ROW_TILE, c), lambda bi, i: (bi, i, 0))
    full = lambda a: pl.BlockSpec(a.shape, lambda bi, i: (0, 0))
    return pl.pallas_call(
        _final_kernel,
        grid=(b, s // ROW_TILE),
        in_specs=[tok_spec(o_c.shape[2]), tok_spec(gate.shape[2]), tok_spec(d), full(w_o),
                  full(final_g)],
        out_specs=tok_spec(d),
        out_shape=jax.ShapeDtypeStruct((b, s, d), F32),
        compiler_params=_cparams(("arbitrary", "arbitrary")),
    )(o_c, gate, h, w_o, final_g)


def _with_ones(v_t):
    return jnp.concatenate([v_t, jnp.ones((BF16_ROWS, v_t.shape[1]), v_t.dtype)], axis=0)


def _softmax_pv(s2, v_aug, m_ref, acc_ref, shift, split):
    m_old = m_ref[...]
    if split:
        half = s2.shape[0] // 2
        s_a, s_b = s2[:half], s2[half:]
        m_a = jnp.maximum(m_old, jnp.max(s_a, axis=0, keepdims=True) + shift)
        p_a = jnp.exp2(s_a - (m_a - shift)).astype(v_aug.dtype)
        m_new = jnp.maximum(m_a, jnp.max(s_b, axis=0, keepdims=True) + shift)
        p_b = jnp.exp2(s_b - (m_new - shift)).astype(v_aug.dtype)
        p = jnp.concatenate([p_a * jnp.exp2(m_a - m_new).astype(v_aug.dtype), p_b], axis=0)
    else:
        m_new = jnp.maximum(m_old, jnp.max(s2, axis=0, keepdims=True) + shift)
        p = jnp.exp2(s2 - (m_new - shift)).astype(v_aug.dtype)
    alpha = jnp.exp2(m_old - m_new)
    acc_ref[...] = alpha * acc_ref[...] + jnp.dot(v_aug, p, preferred_element_type=F32)
    m_ref[...] = m_new


def _init_flash(m_ref, acc_ref):
    m_ref[...] = jnp.full(m_ref.shape, -jnp.inf, F32)
    acc_ref[...] = jnp.zeros(acc_ref.shape, F32)


def _normalised(acc, dv):
    return acc[:dv] / acc[dv:dv + 1]


def _staggered(jobs, lookahead):
    lookahead = len(jobs) if lookahead is None else lookahead
    scores = [None] * len(jobs)
    for i in range(min(lookahead, len(jobs))):
        scores[i] = jobs[i][0]()
    for i, (_, consume) in enumerate(jobs):
        if i + lookahead < len(jobs):
            scores[i + lookahead] = jobs[i + lookahead][0]()
        consume(scores[i])
        scores[i] = None


def _key_tile_loop(qt, jobs_fn, lookahead, has_near):
    n_plain = jnp.maximum(qt - 1, 0) if has_near else qt

    def pair_body(i, carry):
        _staggered(jobs_fn(2 * i, None) + jobs_fn(2 * i + 1, None), lookahead)
        return carry
    lax.fori_loop(0, n_plain >> 1, pair_body, 0)
    odd = (n_plain & 1) == 1
    if has_near:
        @pl.when(odd)
        def _():
            _staggered(jobs_fn(n_plain - 1, None), lookahead)

        @pl.when(qt >= 1)
        def _():
            _staggered(jobs_fn(qt - 1, 1) + jobs_fn(qt, 0), lookahead)

        @pl.when(qt == 0)
        def _():
            _staggered(jobs_fn(qt, 0), lookahead)
    else:
        @pl.when(odd)
        def _():
            _staggered(jobs_fn(qt - 1, None) + jobs_fn(qt, 0), lookahead)

        @pl.when(jnp.logical_not(odd))
        def _():
            _staggered(jobs_fn(qt, 0), lookahead)


def _attn_a_kernel(cfar_ref, q_ref, k_ref, v_ref, bias_ref, lam_ref, g_ref, o_ref,
                   m_ref, acc_ref, *, lam_init):
    qt = pl.program_id(1)
    t = ATT_TILE
    _init_flash(m_ref, acc_ref)

    def jobs(j, bias_idx):
        k0 = pl.multiple_of(j * t, t)

        def make(c):
            h = c // 2

            def qk():
                return jnp.dot(k_ref[0, pl.ds(k0, t), h * LANES:(h + 1) * LANES],
                               q_ref[0, c * LANES:(c + 1) * LANES, :],
                               preferred_element_type=F32)

            def consume(score):
                v_aug = _with_ones(v_ref[0, h * A_V_DIM:(h + 1) * A_V_DIM, pl.ds(k0, t)])
                if bias_idx is None:
                    s2, shift = score, cfar_ref[h]
                else:
                    s2, shift = bias_ref[h, bias_idx] + score, 0.0
                _softmax_pv(s2, v_aug, m_ref.at[c], acc_ref.at[c], shift, split=True)
            return qk, consume
        return [make(c) for c in range(2 * A_HEADS)]

    _key_tile_loop(qt, jobs, None, has_near=True)

    lam_v = lam_ref[...]
    lam = (jnp.exp(jnp.sum(lam_v[0:1] * lam_v[1:2], axis=-1, keepdims=True))
           - jnp.exp(jnp.sum(lam_v[2:3] * lam_v[3:4], axis=-1, keepdims=True)) + lam_init)
    outs = []
    for h in range(A_HEADS):
        o = (_normalised(acc_ref[2 * h], A_V_DIM)
             - lam * _normalised(acc_ref[2 * h + 1], A_V_DIM))
        inv = lax.rsqrt(jnp.mean(o * o, axis=0, keepdims=True) + RMS_EPS)
        outs.append(o * inv * (g_ref[...] * (1.0 - lam_init)))
    o_ref[0] = jnp.concatenate(outs, axis=0).T.astype(o_ref.dtype)


def _attn_a(cfar, qa_t, ka, va_t, bias, lam_vecs, subln_col, lam_init):
    b, s, _ = ka.shape
    nq = s // ATT_TILE
    chains = 2 * A_HEADS
    return pl.pallas_call(
        functools.partial(_attn_a_kernel, lam_init=lam_init),
        grid=(b, nq),
        in_specs=[pl.BlockSpec(memory_space=pltpu.SMEM),
                  pl.BlockSpec((1, chains * LANES, ATT_TILE), lambda bi, qi: (bi, 0, qi)),
                  pl.BlockSpec((1, s, ka.shape[2]), lambda bi, qi: (bi, 0, 0)),
                  pl.BlockSpec((1, va_t.shape[1], s), lambda bi, qi: (bi, 0, 0)),
                  pl.BlockSpec(bias.shape, lambda bi, qi: (0, 0, 0, 0)),
                  pl.BlockSpec(lam_vecs.shape, lambda bi, qi: (0, 0)),
                  pl.BlockSpec(subln_col.shape, lambda bi, qi: (0, 0))],
        out_specs=pl.BlockSpec((1, ATT_TILE, A_HEADS * A_V_DIM), lambda bi, qi: (bi, qi, 0)),
        out_shape=jax.ShapeDtypeStruct((b, s, A_HEADS * A_V_DIM), BF16),
        scratch_shapes=[pltpu.VMEM((chains, 1, ATT_TILE), F32),
                        pltpu.VMEM((chains, A_V_DIM + BF16_ROWS, ATT_TILE), F32)],
        compiler_params=_cparams(("arbitrary", "arbitrary")),
    )(cfar, qa_t, ka, va_t, bias, lam_vecs, subln_col)


def _key_value(key):
    bits = jnp.where(key < 0, (key - 1) ^ 0x7FFFFFFF, key)
    return lax.bitcast_convert_type(bits, F32)


def _attn_b_kernel(cfar_ref, qi_ref, ki_ref, wi_ref, qb_ref, kb_ref, vb_ref, bias_ref, o_ref,
                   keys_ref, seen_ref, thr_ref, ngt_ref, m_ref, acc_ref, *, top_k):
    qt = pl.program_id(1)
    n_tiles = qt + 1
    t = ATT_TILE
    chunk_ok = _chunk_visible_t((t, t))

    def tile_start(j):
        return pl.multiple_of(j * t, t)

    w = wi_ref[0]

    def score_tile(j, carry):
        k0 = tile_start(j)
        kt = ki_ref[0, pl.ds(k0, t), :]
        logits = [jnp.dot(kt, qi_ref[0, hh * IDX_DIM:(hh + 1) * IDX_DIM, :],
                          preferred_element_type=F32) for hh in range(IDX_HEADS)]
        score = None
        for hh in range(IDX_HEADS):
            term = w[hh:hh + 1, :] * jnp.maximum(logits[hh], 0.0)
            score = term if score is None else score + term
        keys_ref[pl.ds(k0, t), :] = jnp.where(chunk_ok | (j < qt), score, -jnp.inf)
        return carry

    def score_pair(i, carry):
        score_tile(2 * i, carry)
        return score_tile(2 * i + 1, carry)

    lax.fori_loop(0, n_tiles >> 1, score_pair, 0)

    @pl.when((n_tiles & 1) == 1)
    def _():
        score_tile(n_tiles - 1, 0)

    def count(pred_fn):
        def body(j, acc):
            hit = pred_fn(keys_ref[pl.ds(tile_start(j), t), :])
            return acc + jnp.sum(hit.reshape(t // COUNT_ROWS, COUNT_ROWS, t), axis=0)
        acc = lax.fori_loop(0, n_tiles, body, jnp.zeros((COUNT_ROWS, t), F32))
        return jnp.sum(acc, axis=0, keepdims=True)

    kf = float(top_k)

    def value_step(i, carry):
        thr, n_ge = carry
        cand = thr + lax.shift_left(jnp.int32(1), jnp.asarray(31 - i, jnp.int32))
        cand_v = _key_value(cand)
        cnt = count(lambda blk: jnp.where(blk >= cand_v, 1.0, 0.0))
        keep = cnt >= kf
        return jnp.where(keep, cand, thr), jnp.where(keep, cnt, n_ge)

    def n_above(thr):
        thr_v = _key_value(thr)
        return count(lambda blk: jnp.where(blk > thr_v, 1.0, 0.0))

    start = (jnp.full((1, t), INT_MIN, jnp.int32),
             jnp.broadcast_to((n_tiles * t).astype(F32), (1, t)))
    thr, n_ge = lax.fori_loop(0, EARLY_BITS, value_step, start)
    n_gt = n_above(thr)
    thr_ref[...] = thr
    ngt_ref[...] = n_gt
    settled = jnp.where((n_ge == kf) | (n_gt < kf), 1.0, 0.0)

    @pl.when(jnp.min(settled) < 1.0)
    def _():
        thr_full, _ = lax.fori_loop(EARLY_BITS, 32, value_step, (thr, n_ge))
        thr_ref[...] = thr_full
        ngt_ref[...] = n_above(thr_full)

    thr = _key_value(thr_ref[...])
    need = kf - ngt_ref[...]
    tri = jnp.where(lax.broadcasted_iota(jnp.int32, (t, t), 0)
                    >= lax.broadcasted_iota(jnp.int32, (t, t), 1), 1.0, 0.0).astype(BF16)

    _init_flash(m_ref, acc_ref)
    seen_ref[...] = jnp.zeros(seen_ref.shape, F32)

    def jobs(j, bias_idx):
        k0 = tile_start(j)
        tile = {}

        def rank_matmul():
            eq = jnp.where(keys_ref[pl.ds(k0, t), :] == thr, 1.0, 0.0)
            return jnp.dot(tri, eq.astype(BF16), preferred_element_type=F32)

        def make_mask(in_tile_rank):
            blk = keys_ref[pl.ds(k0, t), :]
            rank = seen_ref[...] + in_tile_rank
            seen_ref[...] = rank[t - 1:t, :]
            tie_keep = jnp.where(rank <= need, 0.0, NEG)
            tile["mask"] = jnp.where(blk > thr, 0.0, jnp.where(blk == thr, tie_keep, NEG))

        def make(hh):
            def qk():
                return jnp.dot(kb_ref[0, pl.ds(k0, t), :],
                               qb_ref[0, hh * B_HEAD_DIM:(hh + 1) * B_HEAD_DIM, :],
                               preferred_element_type=F32)

            def consume(score):
                v_aug = _with_ones(vb_ref[0, :, pl.ds(k0, t)])
                if bias_idx is None:
                    s2, shift = tile["mask"] + score, cfar_ref[hh]
                else:
                    s2, shift = (tile["mask"] + bias_ref[hh, bias_idx]) + score, 0.0
                _softmax_pv(s2, v_aug, m_ref.at[hh], acc_ref.at[hh], shift, split=True)
            return qk, consume
        return [(rank_matmul, make_mask)] + [make(hh) for hh in range(B_HEADS)]

    _key_tile_loop(qt, jobs, LOOKAHEAD + 1, has_near=True)

    o_t = jnp.concatenate([_normalised(acc_ref[hh], B_HEAD_DIM) for hh in range(B_HEADS)], axis=0)
    o_ref[0] = o_t.T.astype(o_ref.dtype)


def _attn_b(cfar, qi_t, ki, wi_t, qb_t, kb, vb_t, bias, top_k):
    b, s, _ = kb.shape
    nq = s // ATT_TILE
    qmap = lambda bi, qt: (bi, 0, qt)
    kmap = lambda bi, qt: (bi, 0, 0)
    return pl.pallas_call(
        functools.partial(_attn_b_kernel, top_k=top_k),
        grid=(b, nq),
        in_specs=[pl.BlockSpec(memory_space=pltpu.SMEM),
                  pl.BlockSpec((1, qi_t.shape[1], ATT_TILE), qmap),
                  pl.BlockSpec((1, s, ki.shape[2]), kmap),
                  pl.BlockSpec((1, wi_t.shape[1], ATT_TILE), qmap),
                  pl.BlockSpec((1, qb_t.shape[1], ATT_TILE), qmap),
                  pl.BlockSpec((1, s, kb.shape[2]), kmap),
                  pl.BlockSpec((1, vb_t.shape[1], s), kmap),
                  pl.BlockSpec(bias.shape, lambda bi, qt: (0, 0, 0, 0))],
        out_specs=pl.BlockSpec((1, ATT_TILE, B_HEADS * B_HEAD_DIM), lambda bi, qt: (bi, qt, 0)),
        out_shape=jax.ShapeDtypeStruct((b, s, B_HEADS * B_HEAD_DIM), BF16),
        scratch_shapes=[pltpu.VMEM((s, ATT_TILE), F32),
                        pltpu.VMEM((1, ATT_TILE), F32),
                        pltpu.VMEM((1, ATT_TILE), jnp.int32),
                        pltpu.VMEM((1, ATT_TILE), F32),
                        pltpu.VMEM((B_HEADS, 1, ATT_TILE), F32),
                        pltpu.VMEM((B_HEADS, B_HEAD_DIM + BF16_ROWS, ATT_TILE), F32)],
        compiler_params=_cparams(("arbitrary", "arbitrary")),
    )(cfar, qi_t, ki, wi_t, qb_t, kb, vb_t, bias)


def _attn_c_kernel(q_ref, k_ref, v_ref, o_ref, m_ref, acc_ref):
    qt = pl.program_id(2)
    t = ATT_TILE
    chunk_ok = _chunk_visible_t((t, t))
    _init_flash(m_ref, acc_ref)

    def jobs(j, kind):
        k0 = pl.multiple_of(j * t, t)
        diagonal = kind == 0

        def make(g):
            def qk():
                return jnp.dot(k_ref[0, pl.ds(k0, t), g * LANES:(g + 1) * LANES],
                               q_ref[0, g * LANES:(g + 1) * LANES, :],
                               preferred_element_type=F32)

            def consume(score):
                s2 = jnp.where(chunk_ok, score, NEG) if diagonal else score
                v_aug = _with_ones(v_ref[0, g * C_V:(g + 1) * C_V, pl.ds(k0, t)])
                _softmax_pv(s2, v_aug, m_ref.at[g], acc_ref.at[g], 0.0, split=True)
            return qk, consume
        return [make(g) for g in range(C_GROUP)]

    _key_tile_loop(qt, jobs, LOOKAHEAD, has_near=False)
    o_t = jnp.concatenate([_normalised(acc_ref[g], C_V) for g in range(C_GROUP)], axis=0)
    o_ref[0] = o_t.T.astype(o_ref.dtype)


def _attn_c(q_t, k, v_t):
    b, s, _ = k.shape
    nq = s // ATT_TILE
    groups = C_HEADS // C_GROUP
    return pl.pallas_call(
        _attn_c_kernel,
        grid=(b, groups, nq),
        in_specs=[pl.BlockSpec((1, C_GROUP * LANES, ATT_TILE), lambda bi, p, qi: (bi, p, qi)),
                  pl.BlockSpec((1, s, C_GROUP * LANES), lambda bi, p, qi: (bi, 0, p)),
                  pl.BlockSpec((1, C_GROUP * C_V, s), lambda bi, p, qi: (bi, p, 0))],
        out_specs=pl.BlockSpec((1, ATT_TILE, C_GROUP * C_V), lambda bi, p, qi: (bi, qi, p)),
        out_shape=jax.ShapeDtypeStruct((b, s, C_HEADS * C_V), BF16),
        scratch_shapes=[pltpu.VMEM((C_GROUP, 1, ATT_TILE), F32),
                        pltpu.VMEM((C_GROUP, C_V + BF16_ROWS, ATT_TILE), F32)],
        compiler_params=_cparams(("arbitrary", "arbitrary", "arbitrary")),
    )(q_t, k, v_t)


def _rot_cols(w):
    half = w.shape[-1] // 2
    return jnp.concatenate([-w[..., half:], w[..., :half]], axis=-1)


def _rope_slot(w_rope):
    return jnp.pad(w_rope, ((0, 0), (0, 0), (C_NOPE, LANES - C_NOPE - C_ROPE)))


def _t_bf16(w):
    return w.T.astype(BF16)


def kernel(x, rel_bias, e_norm, e_w_in, e_lam_q1, e_lam_k1, e_lam_q2, e_lam_k2, e_subln, e_w_o,
           o_norm, o_w_in, o_q_norm, o_w_uq, o_kv_norm, o_w_ukv, o_w_o, final_norm):
    b, s, d = x.shape
    assert s % ROW_TILE == 0 and s % ATT_TILE == 0 and ATT_TILE % CHUNK == 0
    assert ATT_TILE >= MAX_DISTANCE, "key tiles two or more before the diagonal must be all-far"
    top_k = min(TOPK_MAX, s // 4)
    assert top_k <= ATT_TILE, "every query must see at least top_k keys inside its own window"

    bias = _bias_tiles(rel_bias)
    cfar = rel_bias[T5_FAR_BUCKET] * LOG2E

    w_in = e_w_in[0]
    sizes = (512, 512, 512, 512, 64, 64, 512, 64, 8, 1024)
    offs = np.concatenate([[0], np.cumsum(sizes)])
    col = lambda i: w_in[:, offs[i]:offs[i + 1]]
    zeros64 = jnp.zeros((d, LANES - B_HEAD_DIM), F32)
    w_tok = jnp.concatenate([col(1), col(4), zeros64, col(7), zeros64, col(9)],
                            axis=1).astype(BF16)
    wqa = col(0).reshape(d, A_HEADS, 2, A_HEAD_DIM)
    zq = jnp.zeros((d, A_HEADS, A_HEAD_DIM), F32)
    wqa = jnp.stack([jnp.concatenate([wqa[:, :, 0], zq], axis=-1),
                     jnp.concatenate([zq, wqa[:, :, 1]], axis=-1)], axis=2)
    feat_w = [wqa.reshape(d, 2 * A_HEADS * LANES), col(2), col(3), col(6), col(5),
              jnp.pad(col(8), ((0, 0), (0, BF16_ROWS - IDX_HEADS)))]
    ka, kb, ki, gate0, qa_t, va_t, qb_t, qi_t, vb_t, wi_t = _proj0(
        x, e_norm[0][None], w_tok, _t_bf16(jnp.concatenate(feat_w, axis=1)),
        [w.shape[1] for w in feat_w], sizes[9], (A_HEAD_DIM ** -0.5) * LOG2E)

    lam_init = 0.8 - 0.6 * math.exp(-0.3 * 0)
    lam_vecs = jnp.stack([e_lam_q1[0], e_lam_k1[0], e_lam_q2[0], e_lam_k2[0]])
    o_a = _attn_a(cfar[:A_HEADS], qa_t, ka, va_t, bias[:A_HEADS], lam_vecs,
                  e_subln[0][:, None], lam_init)
    o_b = _attn_b(cfar[A_HEADS:], qi_t, ki, wi_t, qb_t, kb, vb_t, bias[A_HEADS:], top_k)

    w_in1 = o_w_in[0]
    w_ql = w_in1[:, :Q_LORA]
    w_kvl = w_in1[:, Q_LORA:Q_LORA + KV_LORA]
    w_kr = w_in1[:, Q_LORA + KV_LORA:Q_LORA + KV_LORA + C_ROPE]
    w_gate1 = w_in1[:, Q_LORA + KV_LORA + C_ROPE:].astype(BF16)
    kr_slot = _rope_slot(w_kr[:, None, :])[:, 0]
    kr_rot_slot = _rope_slot(_rot_cols(w_kr)[:, None, :])[:, 0]
    w_lat = jnp.concatenate([w_ql, w_kvl, kr_slot, kr_rot_slot], axis=1).astype(BF16)

    w_uq = o_w_uq[0].reshape(Q_LORA, C_HEADS, C_NOPE + C_ROPE)
    w_q = jnp.pad(w_uq[..., :C_NOPE], ((0, 0), (0, 0), (0, LANES - C_NOPE))) \
        + _rope_slot(w_uq[..., C_NOPE:])
    w_qrot = _rope_slot(_rot_cols(w_uq[..., C_NOPE:]))
    w_ukv = o_w_ukv[0].reshape(KV_LORA, C_HEADS, C_NOPE + C_V)
    w_k = jnp.pad(w_ukv[..., :C_NOPE], ((0, 0), (0, 0), (0, LANES - C_NOPE)))
    w_v = w_ukv[..., C_NOPE:]
    flat = lambda w: w.reshape(w.shape[0], -1)

    inv = ROPE_THETA ** (-jnp.arange(0, C_ROPE, 2, dtype=F32) / C_ROPE)
    ang = jnp.arange(s, dtype=F32)[:, None] * inv[None, :]
    cos, sin = jnp.cos(ang), jnp.sin(ang)
    pad_hi = jnp.zeros((s, LANES - C_NOPE - C_ROPE), F32)
    cos_tab = jnp.concatenate([jnp.ones((s, C_NOPE), F32), cos, cos, pad_hi], axis=1)
    sin_tab = jnp.concatenate([jnp.zeros((s, C_NOPE), F32), sin, sin, pad_hi], axis=1)

    h1, q1_t, k1, v1_t, gate1 = _mid(
        o_a, o_b, gate0, x, e_w_o[0].astype(BF16), o_norm[0][None], cos_tab, sin_tab, cos_tab.T,
        sin_tab.T, jnp.concatenate([w_lat, w_gate1], axis=1), w_gate1.shape[1],
        o_q_norm[0][None], o_kv_norm[0][None],
        _t_bf16(jnp.concatenate([flat(w_q), flat(w_qrot)], axis=1)),
        flat(w_k).astype(BF16), _t_bf16(flat(w_v)), ((C_NOPE + C_ROPE) ** -0.5) * LOG2E)
    o_c = _attn_c(q1_t, k1, v1_t)
    return _final(o_c, gate1, h1, o_w_o[0].astype(BF16), final_norm[None])
```

```python
import functools
import math

import numpy as np
import jax
import jax.numpy as jnp
from jax import lax
from jax.experimental import pallas as pl
from jax.experimental.pallas import tpu as pltpu

F32 = jnp.float32
BF16 = jnp.bfloat16

CHUNK = 64
RMS_EPS = 1e-6
N_BUCKETS = 32
MAX_DISTANCE = 128
A_HEADS = 4
A_HEAD_DIM = 64
A_V_DIM = 2 * A_HEAD_DIM
B_HEADS = 8
B_HEAD_DIM = 64
IDX_HEADS = 8
IDX_DIM = 64
TOPK_MAX = 256
C_HEADS = 16
C_NOPE = 64
C_ROPE = 32
C_V = 64
Q_LORA = 384
KV_LORA = 256
ROPE_THETA = 10000.0

LANES = 128
BF16_ROWS = 16
ATT_TILE = 256
ROW_TILE = 512
VMEM_LIMIT = 56 * 1024 * 1024

C_GROUP = 8
COUNT_ROWS = 32
EARLY_BITS = 25
LOOKAHEAD = 8
LOG2E = math.log2(math.e)
NEG = -1e30
INT_MIN = -(2 ** 31)
NT_DIMS = (((1,), (1,)), ((), ()))
CHUNK_SHIFT = CHUNK.bit_length() - 1
assert 1 << CHUNK_SHIFT == CHUNK


def _t5_thresholds():
    nb = N_BUCKETS // 2
    max_exact = nb // 2
    n = np.arange(max_exact, MAX_DISTANCE + 1, dtype=np.float64)
    large = max_exact + (np.log(n / max_exact) / math.log(MAX_DISTANCE / max_exact)
                         * (nb - max_exact)).astype(np.int64)
    large = np.minimum(large, nb - 1)
    thr = [int(n[np.argmax(large >= b)]) for b in range(max_exact + 1, nb)]
    return max_exact, thr


T5_MAX_EXACT, T5_THRESHOLDS = _t5_thresholds()
T5_FAR_BUCKET = N_BUCKETS // 2 - 1


def _cparams(sem):
    return pltpu.CompilerParams(dimension_semantics=sem, vmem_limit_bytes=VMEM_LIMIT)


def _rms(x, g):
    return x * lax.rsqrt(jnp.mean(x * x, axis=-1, keepdims=True) + RMS_EPS) * g


def _silu(g):
    return g / (1.0 + jnp.exp(-g))


def _chunk_visible_t(shape):
    key = lax.broadcasted_iota(jnp.int32, shape, 0)
    qry = lax.broadcasted_iota(jnp.int32, shape, 1)
    return (key >> CHUNK_SHIFT) <= (qry >> CHUNK_SHIFT)


def _nt(w_t, x):
    return lax.dot_general(w_t, x, NT_DIMS, preferred_element_type=F32)


def _bias_tile_kernel(tab_ref, o_ref):
    h = pl.program_id(0)
    t = pl.program_id(1)
    shape = (ATT_TILE, ATT_TILE)
    key = lax.broadcasted_iota(jnp.int32, shape, 0)
    qry = lax.broadcasted_iota(jnp.int32, shape, 1)
    rel = key - qry - t * ATT_TILE
    n = jnp.abs(rel)
    large = jnp.full(shape, T5_MAX_EXACT, jnp.int32)
    for thr in T5_THRESHOLDS:
        large = large + jnp.where(n >= thr, 1, 0)
    bucket = jnp.where(rel > 0, N_BUCKETS // 2, 0) + jnp.where(n < T5_MAX_EXACT, n, large)
    val = jnp.full(shape, tab_ref[0, h], F32)
    for b in range(1, N_BUCKETS):
        val = jnp.where(bucket == b, tab_ref[b, h], val)
    visible = _chunk_visible_t(shape) | (t > 0)
    o_ref[0, 0] = jnp.where(visible, val * LOG2E, NEG)


def _bias_tiles(rel_bias):
    heads = rel_bias.shape[1]
    return pl.pallas_call(
        _bias_tile_kernel,
        grid=(heads, 2),
        in_specs=[pl.BlockSpec(memory_space=pltpu.SMEM)],
        out_specs=pl.BlockSpec((1, 1, ATT_TILE, ATT_TILE), lambda h, t: (h, t, 0, 0)),
        out_shape=jax.ShapeDtypeStruct((heads, 2, ATT_TILE, ATT_TILE), F32),
        compiler_params=_cparams(("arbitrary", "arbitrary")),
    )(rel_bias)


def _proj0_kernel(x_ref, g_ref, w_tok_ref, w_feat_ref, ka_ref, kb_ref, ki_ref, gate_ref,
                  qa_ref, va_ref, qb_ref, qi_ref, vb_ref, wi_ref, *, q_scale):
    hn = _rms(x_ref[0], g_ref[...]).astype(BF16)
    tok = jnp.dot(hn, w_tok_ref[...], preferred_element_type=F32)
    ka_ref[0] = tok[:, :512].astype(BF16)
    kb_ref[0] = tok[:, 512:512 + B_HEAD_DIM].astype(BF16)
    ki_ref[0] = tok[:, 512 + LANES:512 + LANES + IDX_DIM].astype(BF16)
    gate_ref[0] = tok[:, 512 + 2 * LANES:].astype(BF16)
    feat = _nt(w_feat_ref[...], hn)
    off = 0
    for o_ref, scale in ((qa_ref, q_scale), (va_ref, None), (qb_ref, q_scale), (qi_ref, None),
                         (vb_ref, None), (wi_ref, None)):
        rows = o_ref.shape[1]
        y = feat[off:off + rows]
        o_ref[0] = (y if scale is None else y * scale).astype(o_ref.dtype)
        off += rows


def _proj0(x, g, w_tok, w_feat_t, feat_rows, gate_cols, q_scale):
    b, s, d = x.shape
    tok = lambda c, dt: (pl.BlockSpec((1, ROW_TILE, c), lambda bi, i: (bi, i, 0)),
                         jax.ShapeDtypeStruct((b, s, c), dt))
    feat = lambda r, dt: (pl.BlockSpec((1, r, ROW_TILE), lambda bi, i: (bi, 0, i)),
                          jax.ShapeDtypeStruct((b, r, s), dt))
    full = lambda a: pl.BlockSpec(a.shape, lambda bi, i: (0, 0))
    assert sum(feat_rows) == w_feat_t.shape[0]
    outs = [tok(512, BF16), tok(B_HEAD_DIM, BF16), tok(IDX_DIM, BF16), tok(gate_cols, BF16)]
    outs += [feat(r, BF16) for r in feat_rows[:-1]] + [feat(feat_rows[-1], F32)]
    return pl.pallas_call(
        functools.partial(_proj0_kernel, q_scale=q_scale),
        grid=(b, s // ROW_TILE),
        in_specs=[pl.BlockSpec((1, ROW_TILE, d), lambda bi, i: (bi, i, 0)), full(g),
                  full(w_tok), full(w_feat_t)],
        out_specs=[o[0] for o in outs],
        out_shape=[o[1] for o in outs],
        compiler_params=_cparams(("arbitrary", "arbitrary")),
    )(x, g, w_tok, w_feat_t)


def _gated_residual(mixes, gate, x, w_ref):
    sg = _silu(gate.astype(F32))
    y = x
    off = 0
    for m in mixes:
        c = m.shape[1]
        a = (m.astype(F32) * sg[:, off:off + c]).astype(BF16)
        y = y + jnp.dot(a, w_ref[off:off + c, :], preferred_element_type=F32)
        off += c
    return y


def _mid_kernel(oa_ref, ob_ref, gate0_ref, x_ref, wo_ref, g_ref, cos_ref, sin_ref, cos_t_ref,
                sin_t_ref, w_tok_ref, qn_ref, kvn_ref, wqq_ref, wk_ref, wv_ref,
                h_ref, q_ref, k_ref, v_ref, gate_ref, *, q_scale):
    h = _gated_residual([oa_ref[0], ob_ref[0]], gate0_ref[0], x_ref[0], wo_ref)
    h_ref[0] = h
    hn = _rms(h, g_ref[...]).astype(BF16)
    tok = jnp.dot(hn, w_tok_ref[...], preferred_element_type=F32)
    q_lat = tok[:, :Q_LORA]
    kv_lat = tok[:, Q_LORA:Q_LORA + KV_LORA]
    kr = tok[:, Q_LORA + KV_LORA:Q_LORA + KV_LORA + LANES]
    kr_rot = tok[:, Q_LORA + KV_LORA + LANES:Q_LORA + KV_LORA + 2 * LANES]
    gate_ref[0] = tok[:, Q_LORA + KV_LORA + 2 * LANES:].astype(BF16)
    k_rope = kr * cos_ref[...] + kr_rot * sin_ref[...]
    qn = _rms(q_lat, qn_ref[...]).astype(BF16)
    kvn = _rms(kv_lat, kvn_ref[...]).astype(BF16)
    cos_t = cos_t_ref[...] * q_scale
    sin_t = sin_t_ref[...] * q_scale
    qq = _nt(wqq_ref[...], qn)
    kn = jnp.dot(kvn, wk_ref[...], preferred_element_type=F32)
    for hd in range(C_HEADS):
        sl = slice(hd * LANES, (hd + 1) * LANES)
        rot = slice((C_HEADS + hd) * LANES, (C_HEADS + hd + 1) * LANES)
        q_ref[0, sl, :] = (qq[sl] * cos_t + qq[rot] * sin_t).astype(BF16)
        k_ref[0, :, sl] = (kn[:, sl] + k_rope).astype(BF16)
    v_ref[0] = _nt(wv_ref[...], kvn).astype(BF16)


def _mid(o_a, o_b, gate0, x, w_o, g, cos, sin, cos_t, sin_t, w_tok, gate_cols, qn_g, kvn_g, wqq_t,
         w_k, wv_t, q_scale):
    b, s, d = x.shape
    tok_spec = lambda c: pl.BlockSpec((1, ROW_TILE, c), lambda bi, i: (bi, i, 0))
    tok = lambda c, dt: (tok_spec(c), jax.ShapeDtypeStruct((b, s, c), dt))
    feat = lambda r, dt: (pl.BlockSpec((1, r, ROW_TILE), lambda bi, i: (bi, 0, i)),
                          jax.ShapeDtypeStruct((b, r, s), dt))
    full = lambda a: pl.BlockSpec(a.shape, lambda bi, i: (0, 0))
    outs = [tok(d, F32), feat(C_HEADS * LANES, BF16), tok(C_HEADS * LANES, BF16),
            feat(C_HEADS * C_V, BF16), tok(gate_cols, BF16)]
    return pl.pallas_call(
        functools.partial(_mid_kernel, q_scale=q_scale),
        grid=(b, s // ROW_TILE),
        in_specs=[tok_spec(o_a.shape[2]), tok_spec(o_b.shape[2]), tok_spec(gate0.shape[2]),
                  tok_spec(d), full(w_o), full(g),
                  pl.BlockSpec((ROW_TILE, LANES), lambda bi, i: (i, 0)),
                  pl.BlockSpec((ROW_TILE, LANES), lambda bi, i: (i, 0)),
                  pl.BlockSpec((LANES, ROW_TILE), lambda bi, i: (0, i)),
                  pl.BlockSpec((LANES, ROW_TILE), lambda bi, i: (0, i)),
                  full(w_tok), full(qn_g), full(kvn_g), full(wqq_t), full(w_k), full(wv_t)],
        out_specs=[o[0] for o in outs],
        out_shape=[o[1] for o in outs],
        compiler_params=_cparams(("arbitrary", "arbitrary")),
    )(o_a, o_b, gate0, x, w_o, g, cos, sin, cos_t, sin_t, w_tok, qn_g, kvn_g, wqq_t, w_k, wv_t)


def _final_kernel(o_ref, gate_ref, h_ref, w_ref, g_ref, out_ref):
    out_ref[0] = _rms(_gated_residual([o_ref[0]], gate_ref[0], h_ref[0], w_ref), g_ref[...])


def _final(o_c, gate, h, w_o, final_g):
    b, s, d = h.shape
    tok_spec = lambda c: pl.BlockSpec((1, ROW_TILE, c), lambda bi, i: (bi, i, 0))
    full = lambda a: pl.BlockSpec(a.shape, lambda bi, i: (0, 0))
    return pl.pallas_call(
        _final_kernel,
        grid=(b, s // ROW_TILE),
        in_specs=[tok_spec(o_c.shape[2]), tok_spec(gate.shape[2]), tok_spec(d), full(w_o),
                  full(final_g)],
        out_specs=tok_spec(d),
        out_shape=jax.ShapeDtypeStruct((b, s, d), F32),
        compiler_params=_cparams(("arbitrary", "arbitrary")),
    )(o_c, gate, h, w_o, final_g)


def _with_ones(v_t):
    return jnp.concatenate([v_t, jnp.ones((BF16_ROWS, v_t.shape[1]), v_t.dtype)], axis=0)


def _softmax_pv(s2, v_aug, m_ref, acc_ref, shift, split):
    m_old = m_ref[...]
    if split:
        half = s2.shape[0] // 2
        s_a, s_b = s2[:half], s2[half:]
        m_a = jnp.maximum(m_old, jnp.max(s_a, axis=0, keepdims=True) + shift)
        p_a = jnp.exp2(s_a - (m_a - shift)).astype(v_aug.dtype)
        m_new = jnp.maximum(m_a, jnp.max(s_b, axis=0, keepdims=True) + shift)
        p_b = jnp.exp2(s_b - (m_new - shift)).astype(v_aug.dtype)
        p = jnp.concatenate([p_a * jnp.exp2(m_a - m_new).astype(v_aug.dtype), p_b], axis=0)
    else:
        m_new = jnp.maximum(m_old, jnp.max(s2, axis=0, keepdims=True) + shift)
        p = jnp.exp2(s2 - (m_new - shift)).astype(v_aug.dtype)
    alpha = jnp.exp2(m_old - m_new)
    acc_ref[...] = alpha * acc_ref[...] + jnp.dot(v_aug, p, preferred_element_type=F32)
    m_ref[...] = m_new


def _init_flash(m_ref, acc_ref):
    m_ref[...] = jnp.full(m_ref.shape, -jnp.inf, F32)
    acc_ref[...] = jnp.zeros(acc_ref.shape, F32)


def _normalised(acc, dv):
    return acc[:dv] / acc[dv:dv + 1]


def _staggered(jobs, lookahead):
    lookahead = len(jobs) if lookahead is None else lookahead
    scores = [None] * len(jobs)
    for i in range(min(lookahead, len(jobs))):
        scores[i] = jobs[i][0]()
    for i, (_, consume) in enumerate(jobs):
        if i + lookahead < len(jobs):
            scores[i + lookahead] = jobs[i + lookahead][0]()
        consume(scores[i])
        scores[i] = None


def _key_tile_loop3(qt, jobs_fn, lookahead, has_near):
    n_plain = jnp.maximum(qt - 1, 0) if has_near else qt
    trips = n_plain // 3
    rem = n_plain - 3 * trips

    def body(i, carry):
        j = 3 * i
        _staggered(jobs_fn(j, None) + jobs_fn(j + 1, None) + jobs_fn(j + 2, None), lookahead)
        return carry
    lax.fori_loop(0, trips, body, 0)

    def tail(r, with_near):
        chains = []
        for d in range(r, 0, -1):
            chains += jobs_fn(n_plain - d, None)
        if with_near:
            chains += jobs_fn(qt - 1, 1)
        _staggered(chains + jobs_fn(qt, 0), lookahead)

    for r in range(3):
        if has_near and r == 0:
            pl.when(qt == 0)(functools.partial(tail, 0, False))
            pl.when((rem == 0) & (qt >= 1))(functools.partial(tail, 0, True))
        else:
            pl.when(rem == r)(functools.partial(tail, r, has_near))


def _attn_a_kernel(cfar_ref, q_ref, k_ref, v_ref, bias_ref, lam_ref, g_ref, o_ref,
                   m_ref, acc_ref, *, lam_init):
    qt = pl.program_id(1)
    t = ATT_TILE
    _init_flash(m_ref, acc_ref)

    def jobs(j, bias_idx):
        k0 = pl.multiple_of(j * t, t)

        def make(c):
            h = c // 2

            def qk():
                return jnp.dot(k_ref[0, pl.ds(k0, t), h * LANES:(h + 1) * LANES],
                               q_ref[0, c * LANES:(c + 1) * LANES, :],
                               preferred_element_type=F32)

            def consume(score):
                v_aug = _with_ones(v_ref[0, h * A_V_DIM:(h + 1) * A_V_DIM, pl.ds(k0, t)])
                if bias_idx is None:
                    s2, shift = score, cfar_ref[h]
                else:
                    s2, shift = bias_ref[h, bias_idx] + score, 0.0
                _softmax_pv(s2, v_aug, m_ref.at[c], acc_ref.at[c], shift, split=True)
            return qk, consume
        return [make(c) for c in range(2 * A_HEADS)]

    _key_tile_loop3(qt, jobs, None, has_near=True)

    lam_v = lam_ref[...]
    lam = (jnp.exp(jnp.sum(lam_v[0:1] * lam_v[1:2], axis=-1, keepdims=True))
           - jnp.exp(jnp.sum(lam_v[2:3] * lam_v[3:4], axis=-1, keepdims=True)) + lam_init)
    outs = []
    for h in range(A_HEADS):
        o = (_normalised(acc_ref[2 * h], A_V_DIM)
             - lam * _normalised(acc_ref[2 * h + 1], A_V_DIM))
        inv = lax.rsqrt(jnp.mean(o * o, axis=0, keepdims=True) + RMS_EPS)
        outs.append(o * inv * (g_ref[...] * (1.0 - lam_init)))
    o_ref[0] = jnp.concatenate(outs, axis=0).T.astype(o_ref.dtype)


def _attn_a(cfar, qa_t, ka, va_t, bias, lam_vecs, subln_col, lam_init):
    b, s, _ = ka.shape
    nq = s // ATT_TILE
    chains = 2 * A_HEADS
    return pl.pallas_call(
        functools.partial(_attn_a_kernel, lam_init=lam_init),
        grid=(b, nq),
        in_specs=[pl.BlockSpec(memory_space=pltpu.SMEM),
                  pl.BlockSpec((1, chains * LANES, ATT_TILE), lambda bi, qi: (bi, 0, qi)),
                  pl.BlockSpec((1, s, ka.shape[2]), lambda bi, qi: (bi, 0, 0)),
                  pl.BlockSpec((1, va_t.shape[1], s), lambda bi, qi: (bi, 0, 0)),
                  pl.BlockSpec(bias.shape, lambda bi, qi: (0, 0, 0, 0)),
                  pl.BlockSpec(lam_vecs.shape, lambda bi, qi: (0, 0)),
                  pl.BlockSpec(subln_col.shape, lambda bi, qi: (0, 0))],
        out_specs=pl.BlockSpec((1, ATT_TILE, A_HEADS * A_V_DIM), lambda bi, qi: (bi, qi, 0)),
        out_shape=jax.ShapeDtypeStruct((b, s, A_HEADS * A_V_DIM), BF16),
        scratch_shapes=[pltpu.VMEM((chains, 1, ATT_TILE), F32),
                        pltpu.VMEM((chains, A_V_DIM + BF16_ROWS, ATT_TILE), F32)],
        compiler_params=_cparams(("arbitrary", "arbitrary")),
    )(cfar, qa_t, ka, va_t, bias, lam_vecs, subln_col)


def _key_value(key):
    bits = jnp.where(key < 0, (key - 1) ^ 0x7FFFFFFF, key)
    return lax.bitcast_convert_type(bits, F32)


def _attn_b_kernel(cfar_ref, qi_ref, ki_ref, wi_ref, qb_ref, kb_ref, vb_ref, bias_ref, o_ref,
                   keys_ref, seen_ref, thr_ref, ngt_ref, m_ref, acc_ref, *, top_k):
    qt = pl.program_id(1)
    n_tiles = qt + 1
    t = ATT_TILE
    chunk_ok = _chunk_visible_t((t, t))

    def tile_start(j):
        return pl.multiple_of(j * t, t)

    w = wi_ref[0]

    def score_tile(j, carry):
        k0 = tile_start(j)
        kt = ki_ref[0, pl.ds(k0, t), :]
        logits = [jnp.dot(kt, qi_ref[0, hh * IDX_DIM:(hh + 1) * IDX_DIM, :],
                          preferred_element_type=F32) for hh in range(IDX_HEADS)]
        score = None
        for hh in range(IDX_HEADS):
            term = w[hh:hh + 1, :] * jnp.maximum(logits[hh], 0.0)
            score = term if score is None else score + term
        keys_ref[pl.ds(k0, t), :] = jnp.where(chunk_ok | (j < qt), score, -jnp.inf)
        return carry

    def score_pair(i, carry):
        score_tile(2 * i, carry)
        return score_tile(2 * i + 1, carry)

    lax.fori_loop(0, n_tiles >> 1, score_pair, 0)

    @pl.when((n_tiles & 1) == 1)
    def _():
        score_tile(n_tiles - 1, 0)

    def count(pred_fn):
        def body(j, acc):
            hit = pred_fn(keys_ref[pl.ds(tile_start(j), t), :])
            return acc + jnp.sum(hit.reshape(t // COUNT_ROWS, COUNT_ROWS, t), axis=0)
        acc = lax.fori_loop(0, n_tiles, body, jnp.zeros((COUNT_ROWS, t), F32))
        return jnp.sum(acc, axis=0, keepdims=True)

    kf = float(top_k)

    def value_step(i, carry):
        thr, n_ge = carry
        cand = thr + lax.shift_left(jnp.int32(1), jnp.asarray(31 - i, jnp.int32))
        cand_v = _key_value(cand)
        cnt = count(lambda blk: jnp.where(blk >= cand_v, 1.0, 0.0))
        keep = cnt >= kf
        return jnp.where(keep, cand, thr), jnp.where(keep, cnt, n_ge)

    def n_above(thr):
        thr_v = _key_value(thr)
        return count(lambda blk: jnp.where(blk > thr_v, 1.0, 0.0))

    start = (jnp.full((1, t), INT_MIN, jnp.int32),
             jnp.broadcast_to((n_tiles * t).astype(F32), (1, t)))
    thr, n_ge = lax.fori_loop(0, EARLY_BITS, value_step, start)
    n_gt = n_above(thr)
    thr_ref[...] = thr
    ngt_ref[...] = n_gt
    settled = jnp.where((n_ge == kf) | (n_gt < kf), 1.0, 0.0)

    @pl.when(jnp.min(settled) < 1.0)
    def _():
        thr_full, _ = lax.fori_loop(EARLY_BITS, 32, value_step, (thr, n_ge))
        thr_ref[...] = thr_full
        ngt_ref[...] = n_above(thr_full)

    thr = _key_value(thr_ref[...])
    need = kf - ngt_ref[...]
    tri = jnp.where(lax.broadcasted_iota(jnp.int32, (t, t), 0)
                    >= lax.broadcasted_iota(jnp.int32, (t, t), 1), 1.0, 0.0).astype(BF16)

    _init_flash(m_ref, acc_ref)
    seen_ref[...] = jnp.zeros(seen_ref.shape, F32)

    def jobs(j, bias_idx):
        k0 = tile_start(j)
        tile = {}

        def rank_matmul():
            eq = jnp.where(keys_ref[pl.ds(k0, t), :] == thr, 1.0, 0.0)
            return jnp.dot(tri, eq.astype(BF16), preferred_element_type=F32)

        def make_mask(in_tile_rank):
            blk = keys_ref[pl.ds(k0, t), :]
            rank = seen_ref[...] + in_tile_rank
            seen_ref[...] = rank[t - 1:t, :]
            tie_keep = jnp.where(rank <= need, 0.0, NEG)
            tile["mask"] = jnp.where(blk > thr, 0.0, jnp.where(blk == thr, tie_keep, NEG))

        def make(hh):
            def qk():
                return jnp.dot(kb_ref[0, pl.ds(k0, t), :],
                               qb_ref[0, hh * B_HEAD_DIM:(hh + 1) * B_HEAD_DIM, :],
                               preferred_element_type=F32)

            def consume(score):
                v_aug = _with_ones(vb_ref[0, :, pl.ds(k0, t)])
                if bias_idx is None:
                    s2, shift = tile["mask"] + score, cfar_ref[hh]
                else:
                    s2, shift = (tile["mask"] + bias_ref[hh, bias_idx]) + score, 0.0
                _softmax_pv(s2, v_aug, m_ref.at[hh], acc_ref.at[hh], shift, split=True)
            return qk, consume
        return [(rank_matmul, make_mask)] + [make(hh) for hh in range(B_HEADS)]

    _key_tile_loop3(qt, jobs, LOOKAHEAD + 1, has_near=True)

    o_t = jnp.concatenate([_normalised(acc_ref[hh], B_HEAD_DIM) for hh in range(B_HEADS)], axis=0)
    o_ref[0] = o_t.T.astype(o_ref.dtype)


def _attn_b(cfar, qi_t, ki, wi_t, qb_t, kb, vb_t, bias, top_k):
    b, s, _ = kb.shape
    nq = s // ATT_TILE
    qmap = lambda bi, qt: (bi, 0, qt)
    kmap = lambda bi, qt: (bi, 0, 0)
    return pl.pallas_call(
        functools.partial(_attn_b_kernel, top_k=top_k),
        grid=(b, nq),
        in_specs=[pl.BlockSpec(memory_space=pltpu.SMEM),
                  pl.BlockSpec((1, qi_t.shape[1], ATT_TILE), qmap),
                  pl.BlockSpec((1, s, ki.shape[2]), kmap),
                  pl.BlockSpec((1, wi_t.shape[1], ATT_TILE), qmap),
                  pl.BlockSpec((1, qb_t.shape[1], ATT_TILE), qmap),
                  pl.BlockSpec((1, s, kb.shape[2]), kmap),
                  pl.BlockSpec((1, vb_t.shape[1], s), kmap),
                  pl.BlockSpec(bias.shape, lambda bi, qt: (0, 0, 0, 0))],
        out_specs=pl.BlockSpec((1, ATT_TILE, B_HEADS * B_HEAD_DIM), lambda bi, qt: (bi, qt, 0)),
        out_shape=jax.ShapeDtypeStruct((b, s, B_HEADS * B_HEAD_DIM), BF16),
        scratch_shapes=[pltpu.VMEM((s, ATT_TILE), F32),
                        pltpu.VMEM((1, ATT_TILE), F32),
                        pltpu.VMEM((1, ATT_TILE), jnp.int32),
                        pltpu.VMEM((1, ATT_TILE), F32),
                        pltpu.VMEM((B_HEADS, 1, ATT_TILE), F32),
                        pltpu.VMEM((B_HEADS, B_HEAD_DIM + BF16_ROWS, ATT_TILE), F32)],
        compiler_params=_cparams(("arbitrary", "arbitrary")),
    )(cfar, qi_t, ki, wi_t, qb_t, kb, vb_t, bias)


def _attn_c_kernel(q_ref, k_ref, v_ref, o_ref, m_ref, acc_ref):
    qt = pl.program_id(2)
    t = ATT_TILE
    chunk_ok = _chunk_visible_t((t, t))
    _init_flash(m_ref, acc_ref)

    def jobs(j, kind):
        k0 = pl.multiple_of(j * t, t)
        diagonal = kind == 0

        def make(g):
            def qk():
                return jnp.dot(k_ref[0, pl.ds(k0, t), g * LANES:(g + 1) * LANES],
                               q_ref[0, g * LANES:(g + 1) * LANES, :],
                               preferred_element_type=F32)

            def consume(score):
                s2 = jnp.where(chunk_ok, score, NEG) if diagonal else score
                v_aug = _with_ones(v_ref[0, g * C_V:(g + 1) * C_V, pl.ds(k0, t)])
                _softmax_pv(s2, v_aug, m_ref.at[g], acc_ref.at[g], 0.0, split=True)
            return qk, consume
        return [make(g) for g in range(C_GROUP)]

    _key_tile_loop3(qt, jobs, LOOKAHEAD, has_near=False)
    o_t = jnp.concatenate([_normalised(acc_ref[g], C_V) for g in range(C_GROUP)], axis=0)
    o_ref[0] = o_t.T.astype(o_ref.dtype)


def _attn_c(q_t, k, v_t):
    b, s, _ = k.shape
    nq = s // ATT_TILE
    groups = C_HEADS // C_GROUP
    return pl.pallas_call(
        _attn_c_kernel,
        grid=(b, groups, nq),
        in_specs=[pl.BlockSpec((1, C_GROUP * LANES, ATT_TILE), lambda bi, p, qi: (bi, p, qi)),
                  pl.BlockSpec((1, s, C_GROUP * LANES), lambda bi, p, qi: (bi, 0, p)),
                  pl.BlockSpec((1, C_GROUP * C_V, s), lambda bi, p, qi: (bi, p, 0))],
        out_specs=pl.BlockSpec((1, ATT_TILE, C_GROUP * C_V), lambda bi, p, qi: (bi, qi, p)),
        out_shape=jax.ShapeDtypeStruct((b, s, C_HEADS * C_V), BF16),
        scratch_shapes=[pltpu.VMEM((C_GROUP, 1, ATT_TILE), F32),
                        pltpu.VMEM((C_GROUP, C_V + BF16_ROWS, ATT_TILE), F32)],
        compiler_params=_cparams(("arbitrary", "arbitrary", "arbitrary")),
    )(q_t, k, v_t)


def _rot_cols(w):
    half = w.shape[-1] // 2
    return jnp.concatenate([-w[..., half:], w[..., :half]], axis=-1)


def _rope_slot(w_rope):
    return jnp.pad(w_rope, ((0, 0), (0, 0), (C_NOPE, LANES - C_NOPE - C_ROPE)))


def _t_bf16(w):
    return w.T.astype(BF16)


def kernel(x, rel_bias, e_norm, e_w_in, e_lam_q1, e_lam_k1, e_lam_q2, e_lam_k2, e_subln, e_w_o,
           o_norm, o_w_in, o_q_norm, o_w_uq, o_kv_norm, o_w_ukv, o_w_o, final_norm):
    b, s, d = x.shape
    assert s % ROW_TILE == 0 and s % ATT_TILE == 0 and ATT_TILE % CHUNK == 0
    assert ATT_TILE >= MAX_DISTANCE, "key tiles two or more before the diagonal must be all-far"
    top_k = min(TOPK_MAX, s // 4)
    assert top_k <= ATT_TILE, "every query must see at least top_k keys inside its own window"

    bias = _bias_tiles(rel_bias)
    cfar = rel_bias[T5_FAR_BUCKET] * LOG2E

    w_in = e_w_in[0]
    sizes = (512, 512, 512, 512, 64, 64, 512, 64, 8, 1024)
    offs = np.concatenate([[0], np.cumsum(sizes)])
    col = lambda i: w_in[:, offs[i]:offs[i + 1]]
    zeros64 = jnp.zeros((d, LANES - B_HEAD_DIM), F32)
    w_tok = jnp.concatenate([col(1), col(4), zeros64, col(7), zeros64, col(9)],
                            axis=1).astype(BF16)
    wqa = col(0).reshape(d, A_HEADS, 2, A_HEAD_DIM)
    zq = jnp.zeros((d, A_HEADS, A_HEAD_DIM), F32)
    wqa = jnp.stack([jnp.concatenate([wqa[:, :, 0], zq], axis=-1),
                     jnp.concatenate([zq, wqa[:, :, 1]], axis=-1)], axis=2)
    feat_w = [wqa.reshape(d, 2 * A_HEADS * LANES), col(2), col(3), col(6), col(5),
              jnp.pad(col(8), ((0, 0), (0, BF16_ROWS - IDX_HEADS)))]
    ka, kb, ki, gate0, qa_t, va_t, qb_t, qi_t, vb_t, wi_t = _proj0(
        x, e_norm[0][None], w_tok, _t_bf16(jnp.concatenate(feat_w, axis=1)),
        [w.shape[1] for w in feat_w], sizes[9], (A_HEAD_DIM ** -0.5) * LOG2E)

    lam_init = 0.8 - 0.6 * math.exp(-0.3 * 0)
    lam_vecs = jnp.stack([e_lam_q1[0], e_lam_k1[0], e_lam_q2[0], e_lam_k2[0]])
    o_a = _attn_a(cfar[:A_HEADS], qa_t, ka, va_t, bias[:A_HEADS], lam_vecs,
                  e_subln[0][:, None], lam_init)
    o_b = _attn_b(cfar[A_HEADS:], qi_t, ki, wi_t, qb_t, kb, vb_t, bias[A_HEADS:], top_k)

    w_in1 = o_w_in[0]
    w_ql = w_in1[:, :Q_LORA]
    w_kvl = w_in1[:, Q_LORA:Q_LORA + KV_LORA]
    w_kr = w_in1[:, Q_LORA + KV_LORA:Q_LORA + KV_LORA + C_ROPE]
    w_gate1 = w_in1[:, Q_LORA + KV_LORA + C_ROPE:].astype(BF16)
    kr_slot = _rope_slot(w_kr[:, None, :])[:, 0]
    kr_rot_slot = _rope_slot(_rot_cols(w_kr)[:, None, :])[:, 0]
    w_lat = jnp.concatenate([w_ql, w_kvl, kr_slot, kr_rot_slot], axis=1).astype(BF16)

    w_uq = o_w_uq[0].reshape(Q_LORA, C_HEADS, C_NOPE + C_ROPE)
    w_q = jnp.pad(w_uq[..., :C_NOPE], ((0, 0), (0, 0), (0, LANES - C_NOPE))) \
        + _rope_slot(w_uq[..., C_NOPE:])
    w_qrot = _rope_slot(_rot_cols(w_uq[..., C_NOPE:]))
    w_ukv = o_w_ukv[0].reshape(KV_LORA, C_HEADS, C_NOPE + C_V)
    w_k = jnp.pad(w_ukv[..., :C_NOPE], ((0, 0), (0, 0), (0, LANES - C_NOPE)))
    w_v = w_ukv[..., C_NOPE:]
    flat = lambda w: w.reshape(w.shape[0], -1)

    inv = ROPE_THETA ** (-jnp.arange(0, C_ROPE, 2, dtype=F32) / C_ROPE)
    ang = jnp.arange(s, dtype=F32)[:, None] * inv[None, :]
    cos, sin = jnp.cos(ang), jnp.sin(ang)
    pad_hi = jnp.zeros((s, LANES - C_NOPE - C_ROPE), F32)
    cos_tab = jnp.concatenate([jnp.ones((s, C_NOPE), F32), cos, cos, pad_hi], axis=1)
    sin_tab = jnp.concatenate([jnp.zeros((s, C_NOPE), F32), sin, sin, pad_hi], axis=1)

    h1, q1_t, k1, v1_t, gate1 = _mid(
        o_a, o_b, gate0, x, e_w_o[0].astype(BF16), o_norm[0][None], cos_tab, sin_tab, cos_tab.T,
        sin_tab.T, jnp.concatenate([w_lat, w_gate1], axis=1), w_gate1.shape[1],
        o_q_norm[0][None], o_kv_norm[0][None],
        _t_bf16(jnp.concatenate([flat(w_q), flat(w_qrot)], axis=1)),
        flat(w_k).astype(BF16), _t_bf16(flat(w_v)), ((C_NOPE + C_ROPE) ** -0.5) * LOG2E)
    o_c = _attn_c(q1_t, k1, v1_t)
    return _final(o_c, gate1, h1, o_w_o[0].astype(BF16), final_norm[None])
```

```python
import functools
import math

import numpy as np
import jax
import jax.numpy as jnp
from jax import lax
from jax.experimental import pallas as pl
from jax.experimental.pallas import tpu as pltpu

F32 = jnp.float32
BF16 = jnp.bfloat16

CHUNK = 64
RMS_EPS = 1e-6
N_BUCKETS = 32
MAX_DISTANCE = 128
A_HEADS = 4
A_HEAD_DIM = 64
A_V_DIM = 2 * A_HEAD_DIM
B_HEADS = 8
B_HEAD_DIM = 64
IDX_HEADS = 8
IDX_DIM = 64
TOPK_MAX = 256
C_HEADS = 16
C_NOPE = 64
C_ROPE = 32
C_V = 64
Q_LORA = 384
KV_LORA = 256
ROPE_THETA = 10000.0

LANES = 128
BF16_ROWS = 16
ATT_TILE = 256
ROW_TILE = 512
VMEM_LIMIT = 56 * 1024 * 1024

C_GROUP = 8
COUNT_ROWS = 32
EARLY_BITS = 25
LOOKAHEAD = 8
LOG2E = math.log2(math.e)
NEG = -1e30
INT_MIN = -(2 ** 31)
NT_DIMS = (((1,), (1,)), ((), ()))
CHUNK_SHIFT = CHUNK.bit_length() - 1
assert 1 << CHUNK_SHIFT == CHUNK


def _t5_thresholds():
    nb = N_BUCKETS // 2
    max_exact = nb // 2
    n = np.arange(max_exact, MAX_DISTANCE + 1, dtype=np.float64)
    large = max_exact + (np.log(n / max_exact) / math.log(MAX_DISTANCE / max_exact)
                         * (nb - max_exact)).astype(np.int64)
    large = np.minimum(large, nb - 1)
    thr = [int(n[np.argmax(large >= b)]) for b in range(max_exact + 1, nb)]
    return max_exact, thr


T5_MAX_EXACT, T5_THRESHOLDS = _t5_thresholds()
T5_FAR_BUCKET = N_BUCKETS // 2 - 1


def _cparams(sem):
    return pltpu.CompilerParams(dimension_semantics=sem, vmem_limit_bytes=VMEM_LIMIT)


def _rms(x, g):
    return x * lax.rsqrt(jnp.mean(x * x, axis=-1, keepdims=True) + RMS_EPS) * g


def _silu(g):
    return g / (1.0 + jnp.exp(-g))


def _chunk_visible_t(shape):
    key = lax.broadcasted_iota(jnp.int32, shape, 0)
    qry = lax.broadcasted_iota(jnp.int32, shape, 1)
    return (key >> CHUNK_SHIFT) <= (qry >> CHUNK_SHIFT)


def _nt(w_t, x):
    return lax.dot_general(w_t, x, NT_DIMS, preferred_element_type=F32)


def _bias_tile_kernel(tab_ref, o_ref):
    h = pl.program_id(0)
    t = pl.program_id(1)
    shape = (ATT_TILE, ATT_TILE)
    key = lax.broadcasted_iota(jnp.int32, shape, 0)
    qry = lax.broadcasted_iota(jnp.int32, shape, 1)
    rel = key - qry - t * ATT_TILE
    n = jnp.abs(rel)
    large = jnp.full(shape, T5_MAX_EXACT, jnp.int32)
    for thr in T5_THRESHOLDS:
        large = large + jnp.where(n >= thr, 1, 0)
    bucket = jnp.where(rel > 0, N_BUCKETS // 2, 0) + jnp.where(n < T5_MAX_EXACT, n, large)
    val = jnp.full(shape, tab_ref[0, h], F32)
    for b in range(1, N_BUCKETS):
        val = jnp.where(bucket == b, tab_ref[b, h], val)
    visible = _chunk_visible_t(shape) | (t > 0)
    o_ref[0, 0] = jnp.where(visible, val * LOG2E, NEG)


def _bias_tiles(rel_bias):
    heads = rel_bias.shape[1]
    return pl.pallas_call(
        _bias_tile_kernel,
        grid=(heads, 2),
        in_specs=[pl.BlockSpec(memory_space=pltpu.SMEM)],
        out_specs=pl.BlockSpec((1, 1, ATT_TILE, ATT_TILE), lambda h, t: (h, t, 0, 0)),
        out_shape=jax.ShapeDtypeStruct((heads, 2, ATT_TILE, ATT_TILE), F32),
        compiler_params=_cparams(("arbitrary", "arbitrary")),
    )(rel_bias)


def _proj0_kernel(x_ref, g_ref, w_tok_ref, w_feat_ref, ka_ref, kb_ref, ki_ref, gate_ref,
                  qa_ref, va_ref, qb_ref, qi_ref, vb_ref, wi_ref, *, q_scale):
    hn = _rms(x_ref[0], g_ref[...]).astype(BF16)
    tok = jnp.dot(hn, w_tok_ref[...], preferred_element_type=F32)
    ka_ref[0] = tok[:, :512].astype(BF16)
    kb_ref[0] = tok[:, 512:512 + B_HEAD_DIM].astype(BF16)
    ki_ref[0] = tok[:, 512 + LANES:512 + LANES + IDX_DIM].astype(BF16)
    gate_ref[0] = tok[:, 512 + 2 * LANES:].astype(BF16)
    feat = _nt(w_feat_ref[...], hn)
    off = 0
    for o_ref, scale in ((qa_ref, q_scale), (va_ref, None), (qb_ref, q_scale), (qi_ref, None),
                         (vb_ref, None), (wi_ref, None)):
        rows = o_ref.shape[1]
        y = feat[off:off + rows]
        o_ref[0] = (y if scale is None else y * scale).astype(o_ref.dtype)
        off += rows


def _proj0(x, g, w_tok, w_feat_t, feat_rows, gate_cols, q_scale):
    b, s, d = x.shape
    tok = lambda c, dt: (pl.BlockSpec((1, ROW_TILE, c), lambda bi, i: (bi, i, 0)),
                         jax.ShapeDtypeStruct((b, s, c), dt))
    feat = lambda r, dt: (pl.BlockSpec((1, r, ROW_TILE), lambda bi, i: (bi, 0, i)),
                          jax.ShapeDtypeStruct((b, r, s), dt))
    full = lambda a: pl.BlockSpec(a.shape, lambda bi, i: (0, 0))
    assert sum(feat_rows) == w_feat_t.shape[0]
    outs = [tok(512, BF16), tok(B_HEAD_DIM, BF16), tok(IDX_DIM, BF16), tok(gate_cols, BF16)]
    outs += [feat(r, BF16) for r in feat_rows[:-1]] + [feat(feat_rows[-1], F32)]
    return pl.pallas_call(
        functools.partial(_proj0_kernel, q_scale=q_scale),
        grid=(b, s // ROW_TILE),
        in_specs=[pl.BlockSpec((1, ROW_TILE, d), lambda bi, i: (bi, i, 0)), full(g),
                  full(w_tok), full(w_feat_t)],
        out_specs=[o[0] for o in outs],
        out_shape=[o[1] for o in outs],
        compiler_params=_cparams(("arbitrary", "arbitrary")),
    )(x, g, w_tok, w_feat_t)


def _gated_residual(mixes, gate, x, w_ref):
    sg = _silu(gate.astype(F32))
    y = x
    off = 0
    for m in mixes:
        c = m.shape[1]
        a = (m.astype(F32) * sg[:, off:off + c]).astype(BF16)
        y = y + jnp.dot(a, w_ref[off:off + c, :], preferred_element_type=F32)
        off += c
    return y


def _mid_kernel(oa_ref, ob_ref, gate0_ref, x_ref, wo_ref, g_ref, cos_ref, sin_ref, cos_t_ref,
                sin_t_ref, w_tok_ref, qn_ref, kvn_ref, wqq_ref, wk_ref, wv_ref,
                h_ref, q_ref, k_ref, v_ref, gate_ref, *, q_scale):
    h = _gated_residual([oa_ref[0], ob_ref[0]], gate0_ref[0], x_ref[0], wo_ref)
    h_ref[0] = h
    hn = _rms(h, g_ref[...]).astype(BF16)
    tok = jnp.dot(hn, w_tok_ref[...], preferred_element_type=F32)
    q_lat = tok[:, :Q_LORA]
    kv_lat = tok[:, Q_LORA:Q_LORA + KV_LORA]
    kr = tok[:, Q_LORA + KV_LORA:Q_LORA + KV_LORA + LANES]
    kr_rot = tok[:, Q_LORA + KV_LORA + LANES:Q_LORA + KV_LORA + 2 * LANES]
    gate_ref[0] = tok[:, Q_LORA + KV_LORA + 2 * LANES:].astype(BF16)
    k_rope = kr * cos_ref[...] + kr_rot * sin_ref[...]
    qn = _rms(q_lat, qn_ref[...]).astype(BF16)
    kvn = _rms(kv_lat, kvn_ref[...]).astype(BF16)
    cos_t = cos_t_ref[...] * q_scale
    sin_t = sin_t_ref[...] * q_scale
    qq = _nt(wqq_ref[...], qn)
    kn = jnp.dot(kvn, wk_ref[...], preferred_element_type=F32)
    for hd in range(C_HEADS):
        sl = slice(hd * LANES, (hd + 1) * LANES)
        rot = slice((C_HEADS + hd) * LANES, (C_HEADS + hd + 1) * LANES)
        q_ref[0, sl, :] = (qq[sl] * cos_t + qq[rot] * sin_t).astype(BF16)
        k_ref[0, :, sl] = (kn[:, sl] + k_rope).astype(BF16)
    v_ref[0] = _nt(wv_ref[...], kvn).astype(BF16)


def _mid(o_a, o_b, gate0, x, w_o, g, cos, sin, cos_t, sin_t, w_tok, gate_cols, qn_g, kvn_g, wqq_t,
         w_k, wv_t, q_scale):
    b, s, d = x.shape
    tok_spec = lambda c: pl.BlockSpec((1, ROW_TILE, c), lambda bi, i: (bi, i, 0))
    tok = lambda c, dt: (tok_spec(c), jax.ShapeDtypeStruct((b, s, c), dt))
    feat = lambda r, dt: (pl.BlockSpec((1, r, ROW_TILE), lambda bi, i: (bi, 0, i)),
                          jax.ShapeDtypeStruct((b, r, s), dt))
    full = lambda a: pl.BlockSpec(a.shape, lambda bi, i: (0, 0))
    outs = [tok(d, F32), feat(C_HEADS * LANES, BF16), tok(C_HEADS * LANES, BF16),
            feat(C_HEADS * C_V, BF16), tok(gate_cols, BF16)]
    return pl.pallas_call(
        functools.partial(_mid_kernel, q_scale=q_scale),
        grid=(b, s // ROW_TILE),
        in_specs=[tok_spec(o_a.shape[2]), tok_spec(o_b.shape[2]), tok_spec(gate0.shape[2]),
                  tok_spec(d), full(w_o), full(g),
                  pl.BlockSpec((ROW_TILE, LANES), lambda bi, i: (i, 0)),
                  pl.BlockSpec((ROW_TILE, LANES), lambda bi, i: (i, 0)),
                  pl.BlockSpec((LANES, ROW_TILE), lambda bi, i: (0, i)),
                  pl.BlockSpec((LANES, ROW_TILE), lambda bi, i: (0, i)),
                  full(w_tok), full(qn_g), full(kvn_g), full(wqq_t), full(w_k), full(wv_t)],
        out_specs=[o[0] for o in outs],
        out_shape=[o[1] for o in outs],
        compiler_params=_cparams(("arbitrary", "arbitrary")),
    )(o_a, o_b, gate0, x, w_o, g, cos, sin, cos_t, sin_t, w_tok, qn_g, kvn_g, wqq_t, w_k, wv_t)


def _final_kernel(o_ref, gate_ref, h_ref, w_ref, g_ref, out_ref):
    out_ref[0] = _rms(_gated_residual([o_ref[0]], gate_ref[0], h_ref[0], w_ref), g_ref[...])


def _final(o_c, gate, h, w_o, final_g):
    b, s, d = h.shape
    tok_spec = lambda c: pl.BlockSpec((1, ROW_TILE, c), lambda bi, i: (bi, i, 0))
    full = lambda a: pl.BlockSpec(a.shape, lambda bi, i: (0, 0))
    return pl.pallas_call(
        _final_kernel,
        grid=(b, s // ROW_TILE),
        in_specs=[tok_spec(o_c.shape[2]), tok_spec(gate.shape[2]), tok_spec(d), full(w_o),
                  full(final_g)],
        out_specs=tok_spec(d),
        out_shape=jax.ShapeDtypeStruct((b, s, d), F32),
        compiler_params=_cparams(("arbitrary", "arbitrary")),
    )(o_c, gate, h, w_o, final_g)


def _with_ones(v_t):
    return jnp.concatenate([v_t, jnp.ones((BF16_ROWS, v_t.shape[1]), v_t.dtype)], axis=0)


def _softmax_pv(s2, v_aug, m_ref, acc_ref, shift, split):
    m_old = m_ref[...]
    if split:
        half = s2.shape[0] // 2
        s_a, s_b = s2[:half], s2[half:]
        m_a = jnp.maximum(m_old, jnp.max(s_a, axis=0, keepdims=True) + shift)
        p_a = jnp.exp2(s_a - (m_a - shift)).astype(v_aug.dtype)
        m_new = jnp.maximum(m_a, jnp.max(s_b, axis=0, keepdims=True) + shift)
        p_b = jnp.exp2(s_b - (m_new - shift)).astype(v_aug.dtype)
        p = jnp.concatenate([p_a * jnp.exp2(m_a - m_new).astype(v_aug.dtype), p_b], axis=0)
    else:
        m_new = jnp.maximum(m_old, jnp.max(s2, axis=0, keepdims=True) + shift)
        p = jnp.exp2(s2 - (m_new - shift)).astype(v_aug.dtype)
    alpha = jnp.exp2(m_old - m_new)
    acc_ref[...] = alpha * acc_ref[...] + jnp.dot(v_aug, p, preferred_element_type=F32)
    m_ref[...] = m_new


def _init_flash(m_ref, acc_ref):
    m_ref[...] = jnp.full(m_ref.shape, -jnp.inf, F32)
    acc_ref[...] = jnp.zeros(acc_ref.shape, F32)


def _normalised(acc, dv):
    return acc[:dv] / acc[dv:dv + 1]


def _staggered(jobs, lookahead):
    lookahead = len(jobs) if lookahead is None else lookahead
    scores = [None] * len(jobs)
    for i in range(min(lookahead, len(jobs))):
        scores[i] = jobs[i][0]()
    for i, (_, consume) in enumerate(jobs):
        if i + lookahead < len(jobs):
            scores[i + lookahead] = jobs[i + lookahead][0]()
        consume(scores[i])
        scores[i] = None


def _key_tile_loop3(qt, jobs_fn, lookahead, has_near, per_block=3):
    n_plain = jnp.maximum(qt - 1, 0) if has_near else qt
    trips = n_plain // per_block
    rem = n_plain - per_block * trips

    def body(i, carry):
        chains = []
        for d in range(per_block):
            chains += jobs_fn(per_block * i + d, None)
        _staggered(chains, lookahead)
        return carry
    lax.fori_loop(0, trips, body, 0)

    def tail(r, with_near):
        chains = []
        for d in range(r, 0, -1):
            chains += jobs_fn(n_plain - d, None)
        if with_near:
            chains += jobs_fn(qt - 1, 1)
        _staggered(chains + jobs_fn(qt, 0), lookahead)

    for r in range(per_block):
        if has_near and r == 0:
            pl.when(qt == 0)(functools.partial(tail, 0, False))
            pl.when((rem == 0) & (qt >= 1))(functools.partial(tail, 0, True))
        else:
            pl.when(rem == r)(functools.partial(tail, r, has_near))


def _attn_a_kernel(cfar_ref, q_ref, k_ref, v_ref, bias_ref, lam_ref, g_ref, o_ref,
                   m_ref, acc_ref, *, lam_init):
    qt = pl.program_id(1)
    t = ATT_TILE
    _init_flash(m_ref, acc_ref)

    def jobs(j, bias_idx):
        k0 = pl.multiple_of(j * t, t)

        def make(c):
            h = c // 2

            def qk():
                return jnp.dot(k_ref[0, pl.ds(k0, t), h * LANES:(h + 1) * LANES],
                               q_ref[0, c * LANES:(c + 1) * LANES, :],
                               preferred_element_type=F32)

            def consume(score):
                v_aug = _with_ones(v_ref[0, h * A_V_DIM:(h + 1) * A_V_DIM, pl.ds(k0, t)])
                if bias_idx is None:
                    s2, shift = score, cfar_ref[h]
                else:
                    s2, shift = bias_ref[h, bias_idx] + score, 0.0
                _softmax_pv(s2, v_aug, m_ref.at[c], acc_ref.at[c], shift, split=True)
            return qk, consume
        return [make(c) for c in range(2 * A_HEADS)]

    _key_tile_loop3(qt, jobs, None, has_near=True)

    lam_v = lam_ref[...]
    lam = (jnp.exp(jnp.sum(lam_v[0:1] * lam_v[1:2], axis=-1, keepdims=True))
           - jnp.exp(jnp.sum(lam_v[2:3] * lam_v[3:4], axis=-1, keepdims=True)) + lam_init)
    outs = []
    for h in range(A_HEADS):
        o = (_normalised(acc_ref[2 * h], A_V_DIM)
             - lam * _normalised(acc_ref[2 * h + 1], A_V_DIM))
        inv = lax.rsqrt(jnp.mean(o * o, axis=0, keepdims=True) + RMS_EPS)
        outs.append(o * inv * (g_ref[...] * (1.0 - lam_init)))
    o_ref[0] = jnp.concatenate(outs, axis=0).T.astype(o_ref.dtype)


def _attn_a(cfar, qa_t, ka, va_t, bias, lam_vecs, subln_col, lam_init):
    b, s, _ = ka.shape
    nq = s // ATT_TILE
    chains = 2 * A_HEADS
    return pl.pallas_call(
        functools.partial(_attn_a_kernel, lam_init=lam_init),
        grid=(b, nq),
        in_specs=[pl.BlockSpec(memory_space=pltpu.SMEM),
                  pl.BlockSpec((1, chains * LANES, ATT_TILE), lambda bi, qi: (bi, 0, qi)),
                  pl.BlockSpec((1, s, ka.shape[2]), lambda bi, qi: (bi, 0, 0)),
                  pl.BlockSpec((1, va_t.shape[1], s), lambda bi, qi: (bi, 0, 0)),
                  pl.BlockSpec(bias.shape, lambda bi, qi: (0, 0, 0, 0)),
                  pl.BlockSpec(lam_vecs.shape, lambda bi, qi: (0, 0)),
                  pl.BlockSpec(subln_col.shape, lambda bi, qi: (0, 0))],
        out_specs=pl.BlockSpec((1, ATT_TILE, A_HEADS * A_V_DIM), lambda bi, qi: (bi, qi, 0)),
        out_shape=jax.ShapeDtypeStruct((b, s, A_HEADS * A_V_DIM), BF16),
        scratch_shapes=[pltpu.VMEM((chains, 1, ATT_TILE), F32),
                        pltpu.VMEM((chains, A_V_DIM + BF16_ROWS, ATT_TILE), F32)],
        compiler_params=_cparams(("arbitrary", "arbitrary")),
    )(cfar, qa_t, ka, va_t, bias, lam_vecs, subln_col)


def _key_value(key):
    bits = jnp.where(key < 0, (key - 1) ^ 0x7FFFFFFF, key)
    return lax.bitcast_convert_type(bits, F32)


def _attn_b_kernel(cfar_ref, qi_ref, ki_ref, wi_ref, qb_ref, kb_ref, vb_ref, bias_ref, o_ref,
                   keys_ref, seen_ref, thr_ref, ngt_ref, m_ref, acc_ref, *, top_k):
    qt = pl.program_id(1)
    n_tiles = qt + 1
    t = ATT_TILE
    chunk_ok = _chunk_visible_t((t, t))

    def tile_start(j):
        return pl.multiple_of(j * t, t)

    w = wi_ref[0]

    def score_tile(j, carry):
        k0 = tile_start(j)
        kt = ki_ref[0, pl.ds(k0, t), :]
        logits = [jnp.dot(kt, qi_ref[0, hh * IDX_DIM:(hh + 1) * IDX_DIM, :],
                          preferred_element_type=F32) for hh in range(IDX_HEADS)]
        score = None
        for hh in range(IDX_HEADS):
            term = w[hh:hh + 1, :] * jnp.maximum(logits[hh], 0.0)
            score = term if score is None else score + term
        keys_ref[pl.ds(k0, t), :] = jnp.where(chunk_ok | (j < qt), score, -jnp.inf)
        return carry

    def score_pair(i, carry):
        score_tile(2 * i, carry)
        return score_tile(2 * i + 1, carry)

    lax.fori_loop(0, n_tiles >> 1, score_pair, 0)

    @pl.when((n_tiles & 1) == 1)
    def _():
        score_tile(n_tiles - 1, 0)

    def count(pred_fn):
        def body(j, acc):
            hit = pred_fn(keys_ref[pl.ds(tile_start(j), t), :])
            return acc + jnp.sum(hit.reshape(t // COUNT_ROWS, COUNT_ROWS, t), axis=0)
        acc = lax.fori_loop(0, n_tiles, body, jnp.zeros((COUNT_ROWS, t), F32))
        return jnp.sum(acc, axis=0, keepdims=True)

    kf = float(top_k)

    def value_step(i, carry):
        thr, n_ge = carry
        cand = thr + lax.shift_left(jnp.int32(1), jnp.asarray(31 - i, jnp.int32))
        cand_v = _key_value(cand)
        cnt = count(lambda blk: jnp.where(blk >= cand_v, 1.0, 0.0))
        keep = cnt >= kf
        return jnp.where(keep, cand, thr), jnp.where(keep, cnt, n_ge)

    def n_above(thr):
        thr_v = _key_value(thr)
        return count(lambda blk: jnp.where(blk > thr_v, 1.0, 0.0))

    start = (jnp.full((1, t), INT_MIN, jnp.int32),
             jnp.broadcast_to((n_tiles * t).astype(F32), (1, t)))
    thr, n_ge = lax.fori_loop(0, EARLY_BITS, value_step, start)
    n_gt = n_above(thr)
    thr_ref[...] = thr
    ngt_ref[...] = n_gt
    settled = jnp.where((n_ge == kf) | (n_gt < kf), 1.0, 0.0)

    @pl.when(jnp.min(settled) < 1.0)
    def _():
        thr_full, _ = lax.fori_loop(EARLY_BITS, 32, value_step, (thr, n_ge))
        thr_ref[...] = thr_full
        ngt_ref[...] = n_above(thr_full)

    thr = _key_value(thr_ref[...])
    need = kf - ngt_ref[...]
    tri = jnp.where(lax.broadcasted_iota(jnp.int32, (t, t), 0)
                    >= lax.broadcasted_iota(jnp.int32, (t, t), 1), 1.0, 0.0).astype(BF16)

    _init_flash(m_ref, acc_ref)
    seen_ref[...] = jnp.zeros(seen_ref.shape, F32)

    def jobs(j, bias_idx):
        k0 = tile_start(j)
        tile = {}

        def rank_matmul():
            eq = jnp.where(keys_ref[pl.ds(k0, t), :] == thr, 1.0, 0.0)
            return jnp.dot(tri, eq.astype(BF16), preferred_element_type=F32)

        def make_mask(in_tile_rank):
            blk = keys_ref[pl.ds(k0, t), :]
            rank = seen_ref[...] + in_tile_rank
            seen_ref[...] = rank[t - 1:t, :]
            tie_keep = jnp.where(rank <= need, 0.0, NEG)
            tile["mask"] = jnp.where(blk > thr, 0.0, jnp.where(blk == thr, tie_keep, NEG))

        def make(hh):
            def qk():
                return jnp.dot(kb_ref[0, pl.ds(k0, t), :],
                               qb_ref[0, hh * B_HEAD_DIM:(hh + 1) * B_HEAD_DIM, :],
                               preferred_element_type=F32)

            def consume(score):
                v_aug = _with_ones(vb_ref[0, :, pl.ds(k0, t)])
                if bias_idx is None:
                    s2, shift = tile["mask"] + score, cfar_ref[hh]
                else:
                    s2, shift = (tile["mask"] + bias_ref[hh, bias_idx]) + score, 0.0
                _softmax_pv(s2, v_aug, m_ref.at[hh], acc_ref.at[hh], shift, split=True)
            return qk, consume
        return [(rank_matmul, make_mask)] + [make(hh) for hh in range(B_HEADS)]

    _key_tile_loop3(qt, jobs, LOOKAHEAD + 1, has_near=True)

    o_t = jnp.concatenate([_normalised(acc_ref[hh], B_HEAD_DIM) for hh in range(B_HEADS)], axis=0)
    o_ref[0] = o_t.T.astype(o_ref.dtype)


def _attn_b(cfar, qi_t, ki, wi_t, qb_t, kb, vb_t, bias, top_k):
    b, s, _ = kb.shape
    nq = s // ATT_TILE
    qmap = lambda bi, qt: (bi, 0, qt)
    kmap = lambda bi, qt: (bi, 0, 0)
    return pl.pallas_call(
        functools.partial(_attn_b_kernel, top_k=top_k),
        grid=(b, nq),
        in_specs=[pl.BlockSpec(memory_space=pltpu.SMEM),
                  pl.BlockSpec((1, qi_t.shape[1], ATT_TILE), qmap),
                  pl.BlockSpec((1, s, ki.shape[2]), kmap),
                  pl.BlockSpec((1, wi_t.shape[1], ATT_TILE), qmap),
                  pl.BlockSpec((1, qb_t.shape[1], ATT_TILE), qmap),
                  pl.BlockSpec((1, s, kb.shape[2]), kmap),
                  pl.BlockSpec((1, vb_t.shape[1], s), kmap),
                  pl.BlockSpec(bias.shape, lambda bi, qt: (0, 0, 0, 0))],
        out_specs=pl.BlockSpec((1, ATT_TILE, B_HEADS * B_HEAD_DIM), lambda bi, qt: (bi, qt, 0)),
        out_shape=jax.ShapeDtypeStruct((b, s, B_HEADS * B_HEAD_DIM), BF16),
        scratch_shapes=[pltpu.VMEM((s, ATT_TILE), F32),
                        pltpu.VMEM((1, ATT_TILE), F32),
                        pltpu.VMEM((1, ATT_TILE), jnp.int32),
                        pltpu.VMEM((1, ATT_TILE), F32),
                        pltpu.VMEM((B_HEADS, 1, ATT_TILE), F32),
                        pltpu.VMEM((B_HEADS, B_HEAD_DIM + BF16_ROWS, ATT_TILE), F32)],
        compiler_params=_cparams(("arbitrary", "arbitrary")),
    )(cfar, qi_t, ki, wi_t, qb_t, kb, vb_t, bias)


def _attn_c_kernel(q_ref, k_ref, v_ref, o_ref, m_ref, acc_ref):
    qt = pl.program_id(2)
    t = ATT_TILE
    chunk_ok = _chunk_visible_t((t, t))
    _init_flash(m_ref, acc_ref)

    def jobs(j, kind):
        k0 = pl.multiple_of(j * t, t)
        diagonal = kind == 0

        def make(g):
            def qk():
                return jnp.dot(k_ref[0, pl.ds(k0, t), g * LANES:(g + 1) * LANES],
                               q_ref[0, g * LANES:(g + 1) * LANES, :],
                               preferred_element_type=F32)

            def consume(score):
                s2 = jnp.where(chunk_ok, score, NEG) if diagonal else score
                v_aug = _with_ones(v_ref[0, g * C_V:(g + 1) * C_V, pl.ds(k0, t)])
                _softmax_pv(s2, v_aug, m_ref.at[g], acc_ref.at[g], 0.0, split=True)
            return qk, consume
        return [make(g) for g in range(C_GROUP)]

    _key_tile_loop3(qt, jobs, LOOKAHEAD, has_near=False, per_block=4)
    o_t = jnp.concatenate([_normalised(acc_ref[g], C_V) for g in range(C_GROUP)], axis=0)
    o_ref[0] = o_t.T.astype(o_ref.dtype)


def _attn_c(q_t, k, v_t):
    b, s, _ = k.shape
    nq = s // ATT_TILE
    groups = C_HEADS // C_GROUP
    return pl.pallas_call(
        _attn_c_kernel,
        grid=(b, groups, nq),
        in_specs=[pl.BlockSpec((1, C_GROUP * LANES, ATT_TILE), lambda bi, p, qi: (bi, p, qi)),
                  pl.BlockSpec((1, s, C_GROUP * LANES), lambda bi, p, qi: (bi, 0, p)),
                  pl.BlockSpec((1, C_GROUP * C_V, s), lambda bi, p, qi: (bi, p, 0))],
        out_specs=pl.BlockSpec((1, ATT_TILE, C_GROUP * C_V), lambda bi, p, qi: (bi, qi, p)),
        out_shape=jax.ShapeDtypeStruct((b, s, C_HEADS * C_V), BF16),
        scratch_shapes=[pltpu.VMEM((C_GROUP, 1, ATT_TILE), F32),
                        pltpu.VMEM((C_GROUP, C_V + BF16_ROWS, ATT_TILE), F32)],
        compiler_params=_cparams(("arbitrary", "arbitrary", "arbitrary")),
    )(q_t, k, v_t)


def _rot_cols(w):
    half = w.shape[-1] // 2
    return jnp.concatenate([-w[..., half:], w[..., :half]], axis=-1)


def _rope_slot(w_rope):
    return jnp.pad(w_rope, ((0, 0), (0, 0), (C_NOPE, LANES - C_NOPE - C_ROPE)))


def _t_bf16(w):
    return w.T.astype(BF16)


def kernel(x, rel_bias, e_norm, e_w_in, e_lam_q1, e_lam_k1, e_lam_q2, e_lam_k2, e_subln, e_w_o,
           o_norm, o_w_in, o_q_norm, o_w_uq, o_kv_norm, o_w_ukv, o_w_o, final_norm):
    b, s, d = x.shape
    assert s % ROW_TILE == 0 and s % ATT_TILE == 0 and ATT_TILE % CHUNK == 0
    assert ATT_TILE >= MAX_DISTANCE, "key tiles two or more before the diagonal must be all-far"
    top_k = min(TOPK_MAX, s // 4)
    assert top_k <= ATT_TILE, "every query must see at least top_k keys inside its own window"

    bias = _bias_tiles(rel_bias)
    cfar = rel_bias[T5_FAR_BUCKET] * LOG2E

    w_in = e_w_in[0]
    sizes = (512, 512, 512, 512, 64, 64, 512, 64, 8, 1024)
    offs = np.concatenate([[0], np.cumsum(sizes)])
    col = lambda i: w_in[:, offs[i]:offs[i + 1]]
    zeros64 = jnp.zeros((d, LANES - B_HEAD_DIM), F32)
    w_tok = jnp.concatenate([col(1), col(4), zeros64, col(7), zeros64, col(9)],
                            axis=1).astype(BF16)
    wqa = col(0).reshape(d, A_HEADS, 2, A_HEAD_DIM)
    zq = jnp.zeros((d, A_HEADS, A_HEAD_DIM), F32)
    wqa = jnp.stack([jnp.concatenate([wqa[:, :, 0], zq], axis=-1),
                     jnp.concatenate([zq, wqa[:, :, 1]], axis=-1)], axis=2)
    feat_w = [wqa.reshape(d, 2 * A_HEADS * LANES), col(2), col(3), col(6), col(5),
              jnp.pad(col(8), ((0, 0), (0, BF16_ROWS - IDX_HEADS)))]
    ka, kb, ki, gate0, qa_t, va_t, qb_t, qi_t, vb_t, wi_t = _proj0(
        x, e_norm[0][None], w_tok, _t_bf16(jnp.concatenate(feat_w, axis=1)),
        [w.shape[1] for w in feat_w], sizes[9], (A_HEAD_DIM ** -0.5) * LOG2E)

    lam_init = 0.8 - 0.6 * math.exp(-0.3 * 0)
    lam_vecs = jnp.stack([e_lam_q1[0], e_lam_k1[0], e_lam_q2[0], e_lam_k2[0]])
    o_a = _attn_a(cfar[:A_HEADS], qa_t, ka, va_t, bias[:A_HEADS], lam_vecs,
                  e_subln[0][:, None], lam_init)
    o_b = _attn_b(cfar[A_HEADS:], qi_t, ki, wi_t, qb_t, kb, vb_t, bias[A_HEADS:], top_k)

    w_in1 = o_w_in[0]
    w_ql = w_in1[:, :Q_LORA]
    w_kvl = w_in1[:, Q_LORA:Q_LORA + KV_LORA]
    w_kr = w_in1[:, Q_LORA + KV_LORA:Q_LORA + KV_LORA + C_ROPE]
    w_gate1 = w_in1[:, Q_LORA + KV_LORA + C_ROPE:].astype(BF16)
    kr_slot = _rope_slot(w_kr[:, None, :])[:, 0]
    kr_rot_slot = _rope_slot(_rot_cols(w_kr)[:, None, :])[:, 0]
    w_lat = jnp.concatenate([w_ql, w_kvl, kr_slot, kr_rot_slot], axis=1).astype(BF16)

    w_uq = o_w_uq[0].reshape(Q_LORA, C_HEADS, C_NOPE + C_ROPE)
    w_q = jnp.pad(w_uq[..., :C_NOPE], ((0, 0), (0, 0), (0, LANES - C_NOPE))) \
        + _rope_slot(w_uq[..., C_NOPE:])
    w_qrot = _rope_slot(_rot_cols(w_uq[..., C_NOPE:]))
    w_ukv = o_w_ukv[0].reshape(KV_LORA, C_HEADS, C_NOPE + C_V)
    w_k = jnp.pad(w_ukv[..., :C_NOPE], ((0, 0), (0, 0), (0, LANES - C_NOPE)))
    w_v = w_ukv[..., C_NOPE:]
    flat = lambda w: w.reshape(w.shape[0], -1)

    inv = ROPE_THETA ** (-jnp.arange(0, C_ROPE, 2, dtype=F32) / C_ROPE)
    ang = jnp.arange(s, dtype=F32)[:, None] * inv[None, :]
    cos, sin = jnp.cos(ang), jnp.sin(ang)
    pad_hi = jnp.zeros((s, LANES - C_NOPE - C_ROPE), F32)
    cos_tab = jnp.concatenate([jnp.ones((s, C_NOPE), F32), cos, cos, pad_hi], axis=1)
    sin_tab = jnp.concatenate([jnp.zeros((s, C_NOPE), F32), sin, sin, pad_hi], axis=1)

    h1, q1_t, k1, v1_t, gate1 = _mid(
        o_a, o_b, gate0, x, e_w_o[0].astype(BF16), o_norm[0][None], cos_tab, sin_tab, cos_tab.T,
        sin_tab.T, jnp.concatenate([w_lat, w_gate1], axis=1), w_gate1.shape[1],
        o_q_norm[0][None], o_kv_norm[0][None],
        _t_bf16(jnp.concatenate([flat(w_q), flat(w_qrot)], axis=1)),
        flat(w_k).astype(BF16), _t_bf16(flat(w_v)), ((C_NOPE + C_ROPE) ** -0.5) * LOG2E)
    o_c = _attn_c(q1_t, k1, v1_t)
    return _final(o_c, gate1, h1, o_w_o[0].astype(BF16), final_norm[None])
```
